```python
import jax, jax.numpy as jnp
from jax import lax
import numpy as np

D_MODEL = 1024
BATCH = 2
SEQ = 8192
DEPTH = 4
DEC_BATCH = 8
DEC_SEQ = 2048
PAST_LEN = 128

N_MIXERS = 2
N_CONV_LAYERS = (DEPTH + 1) // 2
N_ATTN_LAYERS = DEPTH // 2
CONV_WIDTH = 31
N_HEADS = 16
HEAD_DIM = D_MODEL // N_HEADS
N_KV_HEADS = 4
GROUP = N_HEADS // N_KV_HEADS
WINDOW = 128
BLOCK = 128
D_FF = 2752
FFN_CONV_WIDTH = 3
LN_EPS = 1e-5
NEG_BIG = -1e30
DEEPNORM_ALPHA = (2.0 * DEPTH) ** 0.25
DEEPNORM_BETA = (8.0 * DEPTH) ** -0.25

kernel_name = "hybrid_conformer_swa_convffn_encoder"


def layer_norm(x, g, b):
    xf = x.astype(jnp.float32)
    mu = jnp.mean(xf, -1, keepdims=True)
    var = jnp.mean(jnp.square(xf - mu), -1, keepdims=True)
    y = (xf - mu) * lax.rsqrt(var + LN_EPS)
    return (y * g.astype(jnp.float32) + b.astype(jnp.float32)).astype(x.dtype)


def depthwise_conv(x, w, b):
    pad = w.shape[0] // 2
    y = lax.conv_general_dilated(
        x, w[:, None, :].astype(x.dtype), window_strides=(1,), padding=[(pad, pad)],
        dimension_numbers=('NWC', 'WIO', 'NWC'), feature_group_count=x.shape[-1])
    return y + b


def alibi_slopes():
    return jnp.asarray(2.0 ** (-8.0 * np.arange(1, N_HEADS + 1) / N_HEADS), dtype=jnp.float32)


def conformer_conv(x, w_pw1, b_pw1, w_dw, b_dw, g_n, b_n, w_pw2, b_pw2):
    h = x @ w_pw1 + b_pw1
    a, gate = jnp.split(h, 2, axis=-1)
    h = a * jax.nn.sigmoid(gate)
    h = depthwise_conv(h, w_dw, b_dw)
    h = jax.nn.silu(layer_norm(h, g_n, b_n))
    return h @ w_pw2 + b_pw2


def windowed_gqa(x, w_q, w_kv, w_o, sink):
    B, S, _ = x.shape
    nb = S // BLOCK
    q = (x @ w_q).reshape(B, nb, BLOCK, N_KV_HEADS, GROUP, HEAD_DIM)
    kv = (x @ w_kv).reshape(B, S, 2, N_KV_HEADS, HEAD_DIM)
    k, v = kv[:, :, 0], kv[:, :, 1]

    def bands(t):
        tp = jnp.pad(t, ((0, 0), (BLOCK, BLOCK), (0, 0), (0, 0)))
        tp = tp.reshape(B, nb + 2, BLOCK, N_KV_HEADS, HEAD_DIM)
        return jnp.concatenate([tp[:, :-2], tp[:, 1:-1], tp[:, 2:]], axis=2)

    kb, vb = bands(k), bands(v)
    scores = jnp.einsum('bnqgrd,bnkgd->bngrqk', q, kb,
                        preferred_element_type=jnp.float32) * (HEAD_DIM ** -0.5)
    blk = jnp.arange(nb)[:, None] * BLOCK
    q_pos = blk + jnp.arange(BLOCK)[None, :]
    k_pos = blk - BLOCK + jnp.arange(3 * BLOCK)[None, :]
    dist = jnp.abs(q_pos[:, :, None] - k_pos[:, None, :])
    valid = (dist <= WINDOW) & (k_pos[:, None, :] >= 0) & (k_pos[:, None, :] < S)
    slopes = alibi_slopes().reshape(N_KV_HEADS, GROUP)
    scores = jnp.where(valid[None, :, None, None],
                       scores - slopes[None, None, :, :, None, None]
                       * dist[None, :, None, None].astype(jnp.float32),
                       NEG_BIG)
    sink_l = sink.astype(jnp.float32).reshape(N_KV_HEADS, GROUP)[None, None, :, :, None, None]
    m = jnp.maximum(jnp.max(scores, -1, keepdims=True), sink_l)
    p = jnp.exp(scores - m)
    probs = (p / (jnp.sum(p, -1, keepdims=True) + jnp.exp(sink_l - m))).astype(x.dtype)
    out = jnp.einsum('bngrqk,bnkgd->bnqgrd', probs, vb)
    return out.reshape(B, S, N_HEADS * HEAD_DIM) @ w_o


def conv_ffn(x, w_up, w_dw, b_dw, w_down):
    h = depthwise_conv(x @ w_up, w_dw, b_dw)
    g, u = jnp.split(h, 2, axis=-1)
    return (jax.nn.silu(g) * u) @ w_down


def encoder_trunk(x, conv_w_pw1, conv_b_pw1, conv_w_dw, conv_b_dw, conv_norm_g, conv_norm_b,
                  conv_w_pw2, conv_b_pw2, attn_w_q, attn_w_kv, attn_w_o, attn_sink,
                  ffn_w_up, ffn_w_dw, ffn_b_dw, ffn_w_down,
                  ln_mix_g, ln_mix_b, ln_ffn_g, ln_ffn_b):
    for i in range(DEPTH):
        j = i // N_MIXERS
        if i % N_MIXERS == 0:
            mix = conformer_conv(x, conv_w_pw1[j], conv_b_pw1[j], conv_w_dw[j], conv_b_dw[j],
                                 conv_norm_g[j], conv_norm_b[j], conv_w_pw2[j], conv_b_pw2[j])
        else:
            mix = windowed_gqa(x, attn_w_q[j], attn_w_kv[j], attn_w_o[j], attn_sink[j])
        x = layer_norm(DEEPNORM_ALPHA * x + mix, ln_mix_g[i], ln_mix_b[i])
        f = conv_ffn(x, ffn_w_up[i], ffn_w_dw[i], ffn_b_dw[i], ffn_w_down[i])
        x = layer_norm(DEEPNORM_ALPHA * x + f, ln_ffn_g[i], ln_ffn_b[i])
    return x


def setup_inputs(seed: int = 0) -> dict:
    key = jax.random.key(seed)
    ks = jax.random.split(key, 32)
    D, F, KVD = D_MODEL, D_FF, N_KV_HEADS * HEAD_DIM
    nA, nB = N_CONV_LAYERS, N_ATTN_LAYERS

    def nrm(k, shape, scale):
        return jax.random.normal(k, shape, jnp.float32) * scale

    def gain(k, shape):
        return 1.0 + nrm(k, shape, 0.02)

    beta = DEEPNORM_BETA
    w_k = nrm(ks[10], (nB, D, KVD), D ** -0.5)
    w_v = nrm(ks[11], (nB, D, KVD), D ** -0.5 * beta)
    return {
        "x_prompt": nrm(ks[0], (BATCH, SEQ, D), 1.0),
        "x_sample": nrm(ks[1], (DEC_BATCH, DEC_SEQ, D), 1.0),
        "conv_w_pw1": nrm(ks[2], (nA, D, 2 * D), D ** -0.5),
        "conv_b_pw1": nrm(ks[3], (nA, 2 * D), 0.02),
        "conv_w_dw": nrm(ks[4], (nA, CONV_WIDTH, D), CONV_WIDTH ** -0.5),
        "conv_b_dw": nrm(ks[5], (nA, D), 0.02),
        "conv_norm_g": gain(ks[6], (nA, D)),
        "conv_norm_b": nrm(ks[7], (nA, D), 0.02),
        "conv_w_pw2": nrm(ks[8], (nA, D, D), D ** -0.5 * beta),
        "conv_b_pw2": nrm(ks[9], (nA, D), 0.02),
        "attn_w_q": nrm(ks[12], (nB, D, N_HEADS * HEAD_DIM), D ** -0.5),
        "attn_w_kv": jnp.concatenate([w_k, w_v], axis=-1),
        "attn_w_o": nrm(ks[13], (nB, N_HEADS * HEAD_DIM, D), D ** -0.5 * beta),
        "attn_sink": nrm(ks[14], (nB, N_HEADS), 0.5),
        "ffn_w_up": nrm(ks[15], (DEPTH, D, 2 * F), D ** -0.5),
        "ffn_w_dw": nrm(ks[16], (DEPTH, FFN_CONV_WIDTH, 2 * F), FFN_CONV_WIDTH ** -0.5),
        "ffn_b_dw": nrm(ks[17], (DEPTH, 2 * F), 0.02),
        "ffn_w_down": nrm(ks[18], (DEPTH, F, D), F ** -0.5 * beta),
        "ln_mix_g": gain(ks[19], (DEPTH, D)),
        "ln_mix_b": nrm(ks[20], (DEPTH, D), 0.02),
        "ln_ffn_g": gain(ks[21], (DEPTH, D)),
        "ln_ffn_b": nrm(ks[22], (DEPTH, D), 0.02),
    }


def reference(x_prompt, x_sample, conv_w_pw1, conv_b_pw1, conv_w_dw, conv_b_dw, conv_norm_g,
              conv_norm_b, conv_w_pw2, conv_b_pw2, attn_w_q, attn_w_kv, attn_w_o, attn_sink,
              ffn_w_up, ffn_w_dw, ffn_b_dw, ffn_w_down, ln_mix_g, ln_mix_b, ln_ffn_g, ln_ffn_b):
    y_prompt = encoder_trunk(x_prompt, conv_w_pw1, conv_b_pw1, conv_w_dw, conv_b_dw, conv_norm_g,
                             conv_norm_b, conv_w_pw2, conv_b_pw2, attn_w_q, attn_w_kv, attn_w_o,
                             attn_sink, ffn_w_up, ffn_w_dw, ffn_b_dw, ffn_w_down,
                             ln_mix_g, ln_mix_b, ln_ffn_g, ln_ffn_b)
    y_sample = encoder_trunk(x_sample, conv_w_pw1, conv_b_pw1, conv_w_dw, conv_b_dw, conv_norm_g,
                             conv_norm_b, conv_w_pw2, conv_b_pw2, attn_w_q, attn_w_kv, attn_w_o,
                             attn_sink, ffn_w_up, ffn_w_dw, ffn_b_dw, ffn_w_down,
                             ln_mix_g, ln_mix_b, ln_ffn_g, ln_ffn_b)
    return (y_prompt, y_sample)
```

```python
import functools

import jax
import jax.numpy as jnp
import numpy as np
from jax import lax
from jax.experimental import pallas as pl
from jax.experimental.pallas import tpu as pltpu

D_MODEL = 1024
DEPTH = 4
CONV_WIDTH = 31
CONV_HALO = 16
N_HEADS = 16
HEAD_DIM = 64
N_KV_HEADS = 4
GROUP = N_HEADS // N_KV_HEADS
KV_DIM = N_KV_HEADS * HEAD_DIM
BLOCK = 128
D_FF = 2752
FFN_CHUNK = 256
D_FF_PAD = 2816
N_FFN_CHUNKS = D_FF_PAD // FFN_CHUNK
FFN_HALO = 8
LN_EPS = 1e-5
NEG_BIG = -1e30
DEEPNORM_ALPHA = (2.0 * DEPTH) ** 0.25

ROW_TILE = 512
CONV_ROWS = 64
VMEM_LIMIT = 56 * 1024 * 1024

F32 = jnp.float32
BF16 = jnp.bfloat16


def _layer_norm(y, g, b):
    mu = jnp.mean(y, axis=-1, keepdims=True)
    d = y - mu
    var = jnp.mean(d * d, axis=-1, keepdims=True)
    return d * lax.rsqrt(var + LN_EPS) * g + b


def _resident(shape):
    zeros = (0,) * len(shape)
    return pl.BlockSpec(shape, lambda b, i: zeros, pipeline_mode=pl.Buffered(1))


def _params():
    return pltpu.CompilerParams(
        dimension_semantics=("parallel", "arbitrary"),
        vmem_limit_bytes=VMEM_LIMIT)


def _ffn_kernel(xm_ref, xp_ref, xn_ref, wg_ref, wu_ref, cw_ref, wd_ref, ln_ref, o_ref,
                xe_ref, h_ref, *, tm, nt):
    i = pl.program_id(1)
    xm = xm_ref[0]
    prev = jnp.where(i > 0, xp_ref[0], 0.0)
    nxt = jnp.where(i < nt - 1, xn_ref[0], 0.0)
    xe_ref[...] = jnp.concatenate([prev, xm, nxt], axis=0).astype(BF16)
    rows = tm + 2 * FFN_HALO

    def conv3(h, cw, base):
        up = pltpu.roll(h, 1, axis=0)[FFN_HALO:FFN_HALO + tm]
        dn = pltpu.roll(h, rows - 1, axis=0)[FFN_HALO:FFN_HALO + tm]
        mid = h[FFN_HALO:FFN_HALO + tm]
        return (cw[base:base + 1] * up + cw[base + 1:base + 2] * mid
                + cw[base + 2:base + 3] * dn + cw[base + 3:base + 4])

    for j in range(N_FFN_CHUNKS):
        xe = xe_ref[...]
        cw = cw_ref[j]
        g = conv3(jnp.dot(xe, wg_ref[j], preferred_element_type=F32), cw, 0)
        u = conv3(jnp.dot(xe, wu_ref[j], preferred_element_type=F32), cw, 4)
        act = (g * jax.nn.sigmoid(g)) * u
        h_ref[:, j * FFN_CHUNK:(j + 1) * FFN_CHUNK] = act.astype(BF16)

    f = jnp.dot(h_ref[...], wd_ref[...], preferred_element_type=F32)
    o_ref[0] = _layer_norm(DEEPNORM_ALPHA * xm + f, ln_ref[0:1], ln_ref[1:2])


def _ffn_call(x, wg, wu, cw, wd, ln, tm):
    B, S, D = x.shape
    nt = S // tm
    hb = tm // FFN_HALO
    nhb = S // FFN_HALO
    return pl.pallas_call(
        functools.partial(_ffn_kernel, tm=tm, nt=nt),
        grid=(B, nt),
        in_specs=[
            pl.BlockSpec((1, tm, D), lambda b, i: (b, i, 0)),
            pl.BlockSpec((1, FFN_HALO, D), lambda b, i: (b, jnp.maximum(i * hb - 1, 0), 0)),
            pl.BlockSpec((1, FFN_HALO, D), lambda b, i: (b, jnp.minimum((i + 1) * hb, nhb - 1), 0)),
            _resident(wg.shape), _resident(wu.shape), _resident(cw.shape),
            _resident(wd.shape), _resident(ln.shape),
        ],
        out_specs=pl.BlockSpec((1, tm, D), lambda b, i: (b, i, 0)),
        out_shape=jax.ShapeDtypeStruct(x.shape, F32),
        scratch_shapes=[
            pltpu.VMEM((tm + 2 * FFN_HALO, D), BF16),
            pltpu.VMEM((tm, D_FF_PAD), BF16),
        ],
        compiler_params=_params(),
        name="conv_ffn",
    )(x, x, x, wg, wu, cw, wd, ln)


def _ffn_weights(w_up, w_dw, b_dw, w_down):
    pad = D_FF_PAD - D_FF

    def cols(w):
        w = jnp.pad(w, ((0, 0), (0, pad)))
        return w.reshape(w.shape[0], N_FFN_CHUNKS, FFN_CHUNK).transpose(1, 0, 2)

    wg = cols(w_up[:, :D_FF]).astype(BF16)
    wu = cols(w_up[:, D_FF:]).astype(BF16)
    taps = jnp.concatenate([w_dw[:, :D_FF], b_dw[None, :D_FF],
                            w_dw[:, D_FF:], b_dw[None, D_FF:]], axis=0)
    cw = cols(taps)
    wd = jnp.pad(w_down, ((0, pad), (0, 0))).astype(BF16)
    return wg, wu, cw, wd


def _conv_kernel(xm_ref, xp_ref, xn_ref, wa_ref, wgt_ref, dw_ref, w2_ref, vec_ref, o_ref,
                 hg_ref, c_ref, *, tm, nt):
    i = pl.program_id(1)
    xm = xm_ref[0]
    rows = tm + 2 * CONV_HALO
    xe = jnp.concatenate([xp_ref[0], xm, xn_ref[0]], axis=0).astype(BF16)
    for c in range(D_MODEL // 256):
        sl = slice(c * 256, (c + 1) * 256)
        a = jnp.dot(xe, wa_ref[:, sl], preferred_element_type=F32) + vec_ref[0:1, sl]
        gt = jnp.dot(xe, wgt_ref[:, sl], preferred_element_type=F32) + vec_ref[1:2, sl]
        hg_ref[:, sl] = a * jax.nn.sigmoid(gt)

    @pl.when(i == 0)
    def _():
        hg_ref[0:CONV_HALO, :] = jnp.zeros((CONV_HALO, D_MODEL), F32)

    @pl.when(i == nt - 1)
    def _():
        hg_ref[CONV_HALO + tm:rows, :] = jnp.zeros((CONV_HALO, D_MODEL), F32)

    span = CONV_ROWS + 2 * CONV_HALO

    def row_block(rb, carry):
        r0 = pl.multiple_of(rb * CONV_ROWS, CONV_ROWS)
        for c in range(D_MODEL // 128):
            sl = slice(c * 128, (c + 1) * 128)
            col = hg_ref[pl.ds(r0, span), sl]
            acc = jnp.zeros((CONV_ROWS, 128), F32) + vec_ref[2:3, sl]
            for b in range(8):
                sh = col if b == 0 else pltpu.roll(col, span - b, axis=0)
                for a8 in range(0, span, 8):
                    k = a8 + b - 1
                    if 0 <= k < CONV_WIDTH:
                        acc = acc + dw_ref[k:k + 1, sl] * sh[a8:a8 + CONV_ROWS]
            c_ref[pl.ds(r0, CONV_ROWS), sl] = acc
        return carry

    lax.fori_loop(0, tm // CONV_ROWS, row_block, 0)

    y = _layer_norm(c_ref[...], vec_ref[3:4], vec_ref[4:5])
    s = (y * jax.nn.sigmoid(y)).astype(BF16)
    mix = jnp.dot(s, w2_ref[...], preferred_element_type=F32) + vec_ref[5:6]
    o_ref[0] = _layer_norm(DEEPNORM_ALPHA * xm + mix, vec_ref[6:7], vec_ref[7:8])


def _conv_call(x, wa, wgt, dw, w2, vec, tm):
    B, S, D = x.shape
    nt = S // tm
    hb = tm // CONV_HALO
    nhb = S // CONV_HALO
    return pl.pallas_call(
        functools.partial(_conv_kernel, tm=tm, nt=nt),
        grid=(B, nt),
        in_specs=[
            pl.BlockSpec((1, tm, D), lambda b, i: (b, i, 0)),
            pl.BlockSpec((1, CONV_HALO, D), lambda b, i: (b, jnp.maximum(i * hb - 1, 0), 0)),
            pl.BlockSpec((1, CONV_HALO, D), lambda b, i: (b, jnp.minimum((i + 1) * hb, nhb - 1), 0)),
            _resident(wa.shape), _resident(wgt.shape), _resident(dw.shape),
            _resident(w2.shape), _resident(vec.shape),
        ],
        out_specs=pl.BlockSpec((1, tm, D), lambda b, i: (b, i, 0)),
        out_shape=jax.ShapeDtypeStruct(x.shape, F32),
        scratch_shapes=[
            pltpu.VMEM((tm + 2 * CONV_HALO, D), F32),
            pltpu.VMEM((tm, D), F32),
        ],
        compiler_params=_params(),
        name="conformer_conv",
    )(x, x, x, wa, wgt, dw, w2, vec)


def _qkv_kernel(x_ref, wq_ref, wkv_ref, q_ref, k_ref, v_ref):
    x = x_ref[0].astype(BF16)
    q = jnp.dot(x, wq_ref[...], preferred_element_type=F32) * (HEAD_DIM ** -0.5)
    q_ref[0] = q.astype(BF16)
    kv = jnp.dot(x, wkv_ref[...], preferred_element_type=F32)
    k_ref[0] = kv[:, :KV_DIM].astype(BF16)
    v_ref[0] = kv[:, KV_DIM:].astype(BF16)


def _qkv_call(x, wq, wkv, tm):
    B, S, D = x.shape
    return pl.pallas_call(
        _qkv_kernel,
        grid=(B, S // tm),
        in_specs=[
            pl.BlockSpec((1, tm, D), lambda b, i: (b, i, 0)),
            _resident(wq.shape), _resident(wkv.shape),
        ],
        out_specs=[
            pl.BlockSpec((1, tm, D), lambda b, i: (b, i, 0)),
            pl.BlockSpec((1, tm, KV_DIM), lambda b, i: (b, i, 0)),
            pl.BlockSpec((1, tm, KV_DIM), lambda b, i: (b, i, 0)),
        ],
        out_shape=[
            jax.ShapeDtypeStruct((B, S, D), BF16),
            jax.ShapeDtypeStruct((B, S, KV_DIM), BF16),
            jax.ShapeDtypeStruct((B, S, KV_DIM), BF16),
        ],
        compiler_params=_params(),
        name="qkv_proj",
    )(x, wq, wkv)


def _attn_kernel(x_ref, q_ref, km_ref, kp_ref, kn_ref, vm_ref, vp_ref, vn_ref,
                 bias_ref, sink_ref, wo_ref, ln_ref, o_ref, att_ref, *, tm, nt):
    i = pl.program_id(1)
    kcat = jnp.concatenate([kp_ref[0], km_ref[0], kn_ref[0]], axis=0)
    vcat = jnp.concatenate([vp_ref[0], vm_ref[0], vn_ref[0]], axis=0)
    lane_head = lax.broadcasted_iota(jnp.int32, (BLOCK, KV_DIM), 1) // HEAD_DIM
    nblk = tm // BLOCK
    for jb in range(nblk):
        blk = i * nblk + jb
        variant = jnp.where(blk == 0, 0, jnp.where(blk == nt * nblk - 1, 2, 1))
        kband = kcat[jb * BLOCK:(jb + 3) * BLOCK]
        vband = vcat[jb * BLOCK:(jb + 3) * BLOCK]
        for r in range(GROUP):
            qr = q_ref[0, jb * BLOCK:(jb + 1) * BLOCK, r * KV_DIM:(r + 1) * KV_DIM]
            qs = jnp.concatenate(
                [jnp.where(lane_head == g, qr, jnp.zeros_like(qr)) for g in range(N_KV_HEADS)],
                axis=0)
            s = lax.dot_general(qs, kband, (((1,), (1,)), ((), ())),
                                preferred_element_type=F32)
            s = s + bias_ref[variant, r]
            sink = sink_ref[r]
            m = jnp.maximum(jnp.max(s, axis=-1, keepdims=True), sink)
            p = jnp.exp(s - m)
            inv = 1.0 / (jnp.sum(p, axis=-1, keepdims=True) + jnp.exp(sink - m))
            pb = p.astype(BF16)
            out = jnp.zeros((BLOCK, KV_DIM), F32)
            for g in range(N_KV_HEADS):
                rs = slice(g * BLOCK, (g + 1) * BLOCK)
                og = jnp.dot(pb[rs], vband, preferred_element_type=F32) * inv[rs]
                out = jnp.where(lane_head == g, og, out)
            att_ref[jb * BLOCK:(jb + 1) * BLOCK, r * KV_DIM:(r + 1) * KV_DIM] = out.astype(BF16)
    mix = jnp.dot(att_ref[...], wo_ref[...], preferred_element_type=F32)
    o_ref[0] = _layer_norm(DEEPNORM_ALPHA * x_ref[0] + mix, ln_ref[0:1], ln_ref[1:2])


def _attn_call(x, q, k, v, bias, sink, wo, ln, tm):
    B, S, D = x.shape
    nt = S // tm
    hb = tm // BLOCK
    nhb = S // BLOCK
    main = lambda b, i: (b, i, 0)
    prev = lambda b, i: (b, jnp.maximum(i * hb - 1, 0), 0)
    nxt = lambda b, i: (b, jnp.minimum((i + 1) * hb, nhb - 1), 0)
    return pl.pallas_call(
        functools.partial(_attn_kernel, tm=tm, nt=nt),
        grid=(B, nt),
        in_specs=[
            pl.BlockSpec((1, tm, D), main),
            pl.BlockSpec((1, tm, D), main),
            pl.BlockSpec((1, tm, KV_DIM), main),
            pl.BlockSpec((1, BLOCK, KV_DIM), prev),
            pl.BlockSpec((1, BLOCK, KV_DIM), nxt),
            pl.BlockSpec((1, tm, KV_DIM), main),
            pl.BlockSpec((1, BLOCK, KV_DIM), prev),
            pl.BlockSpec((1, BLOCK, KV_DIM), nxt),
            _resident(bias.shape), _resident(sink.shape), _resident(wo.shape), _resident(ln.shape),
        ],
        out_specs=pl.BlockSpec((1, tm, D), main),
        out_shape=jax.ShapeDtypeStruct(x.shape, F32),
        scratch_shapes=[pltpu.VMEM((tm, D), BF16)],
        compiler_params=_params(),
        name="window_attn",
    )(x, q, k, k, k, v, v, v, bias, sink, wo, ln)


def _attn_bias():
    slopes = (2.0 ** (-8.0 * np.arange(1, N_HEADS + 1) / N_HEADS)).astype(np.float32)
    slopes = slopes.reshape(N_KV_HEADS, GROUP)
    qpos = np.arange(BLOCK)[:, None]
    kpos = np.arange(3 * BLOCK)[None, :] - BLOCK
    dist = np.abs(qpos - kpos)
    band = dist <= BLOCK
    out = np.empty((3, GROUP, N_KV_HEADS, BLOCK, 3 * BLOCK), np.float32)
    for variant in range(3):
        valid = band
        if variant == 0:
            valid = valid & (kpos >= 0)
        if variant == 2:
            valid = valid & (kpos < BLOCK)
        for r in range(GROUP):
            for g in range(N_KV_HEADS):
                out[variant, r, g] = np.where(valid, -slopes[g, r] * dist.astype(np.float32),
                                              np.float32(NEG_BIG))
    return jnp.asarray(out.reshape(3, GROUP, N_KV_HEADS * BLOCK, 3 * BLOCK))


def _attn_weights(w_q, w_o, sink):
    wq = w_q.reshape(D_MODEL, N_KV_HEADS, GROUP, HEAD_DIM).transpose(0, 2, 1, 3)
    wq = wq.reshape(D_MODEL, D_MODEL).astype(BF16)
    wo = w_o.reshape(N_KV_HEADS, GROUP, HEAD_DIM, D_MODEL).transpose(1, 0, 2, 3)
    wo = wo.reshape(D_MODEL, D_MODEL).astype(BF16)
    sk = sink.astype(F32).reshape(N_KV_HEADS, GROUP).T
    sk = jnp.repeat(sk, BLOCK, axis=1)[:, :, None]
    return wq, wo, sk


def _trunk(x, layers, tm):
    for kind, mixer, ffn in layers:
        if kind == "conv":
            x = _conv_call(x, *mixer, tm)
        else:
            wq, wkv, wo, sk, bias, ln = mixer
            q, k, v = _qkv_call(x, wq, wkv, tm)
            x = _attn_call(x, q, k, v, bias, sk, wo, ln, tm)
        x = _ffn_call(x, *ffn, tm)
    return x


def kernel(x_prompt, x_sample, conv_w_pw1, conv_b_pw1, conv_w_dw, conv_b_dw, conv_norm_g,
           conv_norm_b, conv_w_pw2, conv_b_pw2, attn_w_q, attn_w_kv, attn_w_o, attn_sink,
           ffn_w_up, ffn_w_dw, ffn_b_dw, ffn_w_down, ln_mix_g, ln_mix_b, ln_ffn_g, ln_ffn_b):
    bias = _attn_bias()
    layers = []
    for i in range(DEPTH):
        j = i // 2
        if i % 2 == 0:
            wa = conv_w_pw1[j][:, :D_MODEL].astype(BF16)
            wgt = conv_w_pw1[j][:, D_MODEL:].astype(BF16)
            dw = jnp.pad(conv_w_dw[j], ((0, 32 - CONV_WIDTH), (0, 0)))
            vec = jnp.stack([conv_b_pw1[j][:D_MODEL], conv_b_pw1[j][D_MODEL:], conv_b_dw[j],
                             conv_norm_g[j], conv_norm_b[j], conv_b_pw2[j],
                             ln_mix_g[i], ln_mix_b[i]], axis=0)
            mixer = (wa, wgt, dw, conv_w_pw2[j].astype(BF16), vec)
            kind = "conv"
        else:
            wq, wo, sk = _attn_weights(attn_w_q[j], attn_w_o[j], attn_sink[j])
            ln = jnp.stack([ln_mix_g[i], ln_mix_b[i]], axis=0)
            mixer = (wq, attn_w_kv[j].astype(BF16), wo, sk, bias, ln)
            kind = "attn"
        ffn = _ffn_weights(ffn_w_up[i], ffn_w_dw[i], ffn_b_dw[i], ffn_w_down[i])
        ffn = ffn + (jnp.stack([ln_ffn_g[i], ln_ffn_b[i]], axis=0),)
        layers.append((kind, mixer, ffn))
    return (_trunk(x_prompt, layers, ROW_TILE), _trunk(x_sample, layers, ROW_TILE))
```

```python
import functools

import jax
import jax.numpy as jnp
import numpy as np
from jax import lax
from jax.experimental import pallas as pl
from jax.experimental.pallas import tpu as pltpu

D_MODEL = 1024
DEPTH = 4
CONV_WIDTH = 31
CONV_HALO = 16
N_HEADS = 16
HEAD_DIM = 64
N_KV_HEADS = 4
GROUP = N_HEADS // N_KV_HEADS
KV_DIM = N_KV_HEADS * HEAD_DIM
BLOCK = 128
D_FF = 2752
FFN_CHUNK = 256
D_FF_PAD = 2816
N_FFN_CHUNKS = D_FF_PAD // FFN_CHUNK
FFN_HALO = 8
LN_EPS = 1e-5
NEG_BIG = -1e30
DEEPNORM_ALPHA = (2.0 * DEPTH) ** 0.25

ROW_TILE = 512
CONV_ROWS = 64
OPROJ_ROWS = 256
VMEM_LIMIT = 56 * 1024 * 1024

F32 = jnp.float32
BF16 = jnp.bfloat16
_NT = (((1,), (1,)), ((), ()))
_TN = (((0,), (0,)), ((), ()))


def _layer_norm(y, g, b):
    mu = jnp.mean(y, axis=-1, keepdims=True)
    d = y - mu
    var = jnp.mean(d * d, axis=-1, keepdims=True)
    return d * lax.rsqrt(var + LN_EPS) * g + b


def _resident(shape):
    zeros = (0,) * len(shape)
    return pl.BlockSpec(shape, lambda b, i: zeros, pipeline_mode=pl.Buffered(1))


def _params():
    return pltpu.CompilerParams(
        dimension_semantics=("parallel", "arbitrary"),
        vmem_limit_bytes=VMEM_LIMIT)


def _ffn_kernel(xm_ref, xp_ref, xn_ref, wg_ref, wu_ref, cw_ref, wd_ref, ln_ref, o_ref,
                xe_ref, h_ref, *, tm, nt):
    i = pl.program_id(1)
    xm = xm_ref[0]
    prev = jnp.where(i > 0, xp_ref[0], 0.0)
    nxt = jnp.where(i < nt - 1, xn_ref[0], 0.0)
    xe_ref[...] = jnp.concatenate([prev, xm, nxt], axis=0).astype(BF16)
    rows = tm + 2 * FFN_HALO

    def conv3(h, cw, base):
        up = pltpu.roll(h, 1, axis=0)[FFN_HALO:FFN_HALO + tm]
        dn = pltpu.roll(h, rows - 1, axis=0)[FFN_HALO:FFN_HALO + tm]
        mid = h[FFN_HALO:FFN_HALO + tm]
        return (cw[base:base + 1] * up + cw[base + 1:base + 2] * mid
                + cw[base + 2:base + 3] * dn + cw[base + 3:base + 4])

    for j in range(N_FFN_CHUNKS):
        xe = xe_ref[...]
        cw = cw_ref[j]
        g = conv3(jnp.dot(xe, wg_ref[j], preferred_element_type=F32), cw, 0)
        u = conv3(jnp.dot(xe, wu_ref[j], preferred_element_type=F32), cw, 4)
        act = (g * jax.nn.sigmoid(g)) * u
        h_ref[:, j * FFN_CHUNK:(j + 1) * FFN_CHUNK] = act.astype(BF16)

    f = jnp.dot(h_ref[...], wd_ref[...], preferred_element_type=F32)
    o_ref[0] = _layer_norm(DEEPNORM_ALPHA * xm + f, ln_ref[0:1], ln_ref[1:2])


def _ffn_call(x, wg, wu, cw, wd, ln, tm):
    B, S, D = x.shape
    nt = S // tm
    hb = tm // FFN_HALO
    nhb = S // FFN_HALO
    return pl.pallas_call(
        functools.partial(_ffn_kernel, tm=tm, nt=nt),
        grid=(B, nt),
        in_specs=[
            pl.BlockSpec((1, tm, D), lambda b, i: (b, i, 0)),
            pl.BlockSpec((1, FFN_HALO, D), lambda b, i: (b, jnp.maximum(i * hb - 1, 0), 0)),
            pl.BlockSpec((1, FFN_HALO, D), lambda b, i: (b, jnp.minimum((i + 1) * hb, nhb - 1), 0)),
            _resident(wg.shape), _resident(wu.shape), _resident(cw.shape),
            _resident(wd.shape), _resident(ln.shape),
        ],
        out_specs=pl.BlockSpec((1, tm, D), lambda b, i: (b, i, 0)),
        out_shape=jax.ShapeDtypeStruct(x.shape, F32),
        scratch_shapes=[
            pltpu.VMEM((tm + 2 * FFN_HALO, D), BF16),
            pltpu.VMEM((tm, D_FF_PAD), BF16),
        ],
        compiler_params=_params(),
        name="conv_ffn",
    )(x, x, x, wg, wu, cw, wd, ln)


def _ffn_weights(w_up, w_dw, b_dw, w_down):
    pad = D_FF_PAD - D_FF

    def cols(w):
        w = jnp.pad(w, ((0, 0), (0, pad)))
        return w.reshape(w.shape[0], N_FFN_CHUNKS, FFN_CHUNK).transpose(1, 0, 2)

    wg = cols(w_up[:, :D_FF]).astype(BF16)
    wu = cols(w_up[:, D_FF:]).astype(BF16)
    taps = jnp.concatenate([w_dw[:, :D_FF], b_dw[None, :D_FF],
                            w_dw[:, D_FF:], b_dw[None, D_FF:]], axis=0)
    cw = cols(taps)
    wd = jnp.pad(w_down, ((0, pad), (0, 0))).astype(BF16)
    return wg, wu, cw, wd


def _conv_kernel(xm_ref, xp_ref, xn_ref, wa_ref, wgt_ref, dw_ref, w2_ref, vec_ref, o_ref,
                 hg_ref, c_ref, *, tm, nt):
    i = pl.program_id(1)
    xm = xm_ref[0]
    rows = tm + 2 * CONV_HALO
    xe = jnp.concatenate([xp_ref[0], xm, xn_ref[0]], axis=0).astype(BF16)
    for c in range(D_MODEL // 256):
        sl = slice(c * 256, (c + 1) * 256)
        a = jnp.dot(xe, wa_ref[:, sl], preferred_element_type=F32) + vec_ref[0:1, sl]
        gt = jnp.dot(xe, wgt_ref[:, sl], preferred_element_type=F32) + vec_ref[1:2, sl]
        hg_ref[:, sl] = a * jax.nn.sigmoid(gt)

    @pl.when(i == 0)
    def _():
        hg_ref[0:CONV_HALO, :] = jnp.zeros((CONV_HALO, D_MODEL), F32)

    @pl.when(i == nt - 1)
    def _():
        hg_ref[CONV_HALO + tm:rows, :] = jnp.zeros((CONV_HALO, D_MODEL), F32)

    span = CONV_ROWS + 2 * CONV_HALO

    def row_block(rb, carry):
        r0 = pl.multiple_of(rb * CONV_ROWS, CONV_ROWS)
        for c in range(D_MODEL // 128):
            sl = slice(c * 128, (c + 1) * 128)
            col = hg_ref[pl.ds(r0, span), sl]
            acc = jnp.zeros((CONV_ROWS, 128), F32) + vec_ref[2:3, sl]
            for b in range(8):
                sh = col if b == 0 else pltpu.roll(col, span - b, axis=0)
                for a8 in range(0, span, 8):
                    k = a8 + b - 1
                    if 0 <= k < CONV_WIDTH:
                        acc = acc + dw_ref[k:k + 1, sl] * sh[a8:a8 + CONV_ROWS]
            c_ref[pl.ds(r0, CONV_ROWS), sl] = acc
        return carry

    lax.fori_loop(0, tm // CONV_ROWS, row_block, 0)

    y = _layer_norm(c_ref[...], vec_ref[3:4], vec_ref[4:5])
    s = (y * jax.nn.sigmoid(y)).astype(BF16)
    mix = jnp.dot(s, w2_ref[...], preferred_element_type=F32) + vec_ref[5:6]
    o_ref[0] = _layer_norm(DEEPNORM_ALPHA * xm + mix, vec_ref[6:7], vec_ref[7:8])


def _conv_call(x, wa, wgt, dw, w2, vec, tm):
    B, S, D = x.shape
    nt = S // tm
    hb = tm // CONV_HALO
    nhb = S // CONV_HALO
    return pl.pallas_call(
        functools.partial(_conv_kernel, tm=tm, nt=nt),
        grid=(B, nt),
        in_specs=[
            pl.BlockSpec((1, tm, D), lambda b, i: (b, i, 0)),
            pl.BlockSpec((1, CONV_HALO, D), lambda b, i: (b, jnp.maximum(i * hb - 1, 0), 0)),
            pl.BlockSpec((1, CONV_HALO, D), lambda b, i: (b, jnp.minimum((i + 1) * hb, nhb - 1), 0)),
            _resident(wa.shape), _resident(wgt.shape), _resident(dw.shape),
            _resident(w2.shape), _resident(vec.shape),
        ],
        out_specs=pl.BlockSpec((1, tm, D), lambda b, i: (b, i, 0)),
        out_shape=jax.ShapeDtypeStruct(x.shape, F32),
        scratch_shapes=[
            pltpu.VMEM((tm + 2 * CONV_HALO, D), F32),
            pltpu.VMEM((tm, D), F32),
        ],
        compiler_params=_params(),
        name="conformer_conv",
    )(x, x, x, wa, wgt, dw, w2, vec)


def _qkv_kernel(x_ref, wqt_ref, wk_ref, wvt_ref, qt_ref, k_ref, vt_ref, *, tm):
    x = x_ref[0].astype(BF16)
    qt = lax.dot_general(wqt_ref[...], x, _NT, preferred_element_type=F32) * (HEAD_DIM ** -0.5)
    qt = qt.astype(BF16)
    vt = lax.dot_general(wvt_ref[...], x, _NT, preferred_element_type=F32).astype(BF16)
    for jb in range(tm // BLOCK):
        qt_ref[0, jb] = qt[:, jb * BLOCK:(jb + 1) * BLOCK]
        vt_ref[0, jb] = vt[:, jb * BLOCK:(jb + 1) * BLOCK]
    k_ref[0] = jnp.dot(x, wk_ref[...], preferred_element_type=F32).astype(BF16)


def _qkv_call(x, wqt, wk, wvt, tm):
    B, S, D = x.shape
    nblk = tm // BLOCK
    return pl.pallas_call(
        functools.partial(_qkv_kernel, tm=tm),
        grid=(B, S // tm),
        in_specs=[
            pl.BlockSpec((1, tm, D), lambda b, i: (b, i, 0)),
            _resident(wqt.shape), _resident(wk.shape), _resident(wvt.shape),
        ],
        out_specs=[
            pl.BlockSpec((1, nblk, D, BLOCK), lambda b, i: (b, i, 0, 0)),
            pl.BlockSpec((1, tm, KV_DIM), lambda b, i: (b, i, 0)),
            pl.BlockSpec((1, nblk, KV_DIM, BLOCK), lambda b, i: (b, i, 0, 0)),
        ],
        out_shape=[
            jax.ShapeDtypeStruct((B, S // BLOCK, D, BLOCK), BF16),
            jax.ShapeDtypeStruct((B, S, KV_DIM), BF16),
            jax.ShapeDtypeStruct((B, S // BLOCK, KV_DIM, BLOCK), BF16),
        ],
        compiler_params=_params(),
        name="qkv_proj",
    )(x, wqt, wk, wvt)


def _attn_kernel(x_ref, qt_ref, km_ref, kp_ref, kn_ref, vm_ref, vp_ref, vn_ref,
                 bias_ref, sink_ref, wo_ref, ln_ref, o_ref,
                 kcat_ref, vblk_ref, att_ref, s0_ref, s1_ref, p0_ref, p1_ref, inv0_ref, inv1_ref,
                 *, tm, nt):
    i = pl.program_id(1)
    nblk = tm // BLOCK
    last_blk = nt * nblk - 1
    kcat_ref[0:BLOCK] = kp_ref[0]
    kcat_ref[BLOCK:BLOCK + tm] = km_ref[0]
    kcat_ref[BLOCK + tm:2 * BLOCK + tm] = kn_ref[0]
    vblk_ref[0] = vp_ref[0, 0]
    for jb in range(nblk):
        vblk_ref[jb + 1] = vm_ref[0, jb]
    vblk_ref[nblk + 1] = vn_ref[0, 0]

    def item(t):
        if isinstance(t, int):
            jb, r = divmod(t, GROUP)
            return jb, r, jb * BLOCK, r * KV_DIM
        jb = t // GROUP
        r = t % GROUP
        return jb, r, pl.multiple_of(jb * BLOCK, BLOCK), pl.multiple_of(r * KV_DIM, KV_DIM)

    def stage_a(t, s_ref):
        jb, _, row0, f0 = item(t)
        qt = qt_ref[0, jb, pl.ds(f0, KV_DIM), :]
        zero = jnp.zeros((HEAD_DIM, BLOCK), BF16)
        qs = jnp.concatenate(
            [jnp.concatenate([qt[g * HEAD_DIM:(g + 1) * HEAD_DIM] if h == g else zero
                              for h in range(N_KV_HEADS)], axis=0)
             for g in range(N_KV_HEADS)], axis=1)
        kband = kcat_ref[pl.ds(row0, 3 * BLOCK), :]
        s_ref[...] = jnp.dot(kband, qs, preferred_element_type=F32)

    def stage_b(t, s_ref, p_ref, inv_ref):
        jb, r, _, _ = item(t)
        blk = i * nblk + jb
        variant = jnp.where(blk == 0, 0, jnp.where(blk == last_blk, 2, 1))
        for g in range(N_KV_HEADS):
            cs = slice(g * BLOCK, (g + 1) * BLOCK)
            s = s_ref[:, cs] + bias_ref[variant, r, :, cs]
            sink = sink_ref[r, :, cs]
            m = jnp.maximum(jnp.max(s, axis=0, keepdims=True), sink)
            p = jnp.exp(s - m)
            inv_ref[:, cs] = 1.0 / (jnp.sum(p, axis=0, keepdims=True) + jnp.exp(sink - m))
            p_ref[:, cs] = p.astype(BF16)

    def stage_c(t, p_ref, inv_ref):
        jb, _, _, f0 = item(t)
        vband = jnp.concatenate([vblk_ref[jb], vblk_ref[jb + 1], vblk_ref[jb + 2]], axis=1)
        o = jnp.dot(vband, p_ref[...], preferred_element_type=F32)
        inv = inv_ref[...]
        out = jnp.concatenate(
            [o[g * HEAD_DIM:(g + 1) * HEAD_DIM, g * BLOCK:(g + 1) * BLOCK]
             * inv[:, g * BLOCK:(g + 1) * BLOCK] for g in range(N_KV_HEADS)], axis=0)
        att_ref[jb, pl.ds(f0, KV_DIM), :] = out.astype(BF16)

    npair = nblk * GROUP // 2
    stage_a(0, s0_ref)
    stage_a(1, s1_ref)
    stage_b(0, s0_ref, p0_ref, inv0_ref)
    stage_b(1, s1_ref, p1_ref, inv1_ref)
    stage_a(2, s0_ref)
    stage_a(3, s1_ref)

    def trip(u, carry):
        t = 2 * u
        stage_c(t - 2, p0_ref, inv0_ref)
        stage_c(t - 1, p1_ref, inv1_ref)
        stage_b(t, s0_ref, p0_ref, inv0_ref)
        stage_b(t + 1, s1_ref, p1_ref, inv1_ref)
        stage_a(t + 2, s0_ref)
        stage_a(t + 3, s1_ref)
        return carry

    lax.fori_loop(1, npair - 1, trip, 0)
    t = 2 * (npair - 1)
    stage_c(t - 2, p0_ref, inv0_ref)
    stage_c(t - 1, p1_ref, inv1_ref)
    stage_b(t, s0_ref, p0_ref, inv0_ref)
    stage_b(t + 1, s1_ref, p1_ref, inv1_ref)
    stage_c(t, p0_ref, inv0_ref)
    stage_c(t + 1, p1_ref, inv1_ref)

    for c in range(tm // OPROJ_ROWS):
        rows = slice(c * OPROJ_ROWS, (c + 1) * OPROJ_ROWS)
        nb = OPROJ_ROWS // BLOCK
        att = jnp.concatenate([att_ref[c * nb + j] for j in range(nb)], axis=1)
        mix = lax.dot_general(att, wo_ref[...], _TN, preferred_element_type=F32)
        o_ref[0, rows] = _layer_norm(DEEPNORM_ALPHA * x_ref[0, rows] + mix,
                                     ln_ref[0:1], ln_ref[1:2])


def _attn_call(x, qt, k, vt, bias, sink, wo, ln, tm):
    B, S, D = x.shape
    nt = S // tm
    nblk = tm // BLOCK
    nsb = S // BLOCK
    main = lambda b, i: (b, i, 0)
    kprev = lambda b, i: (b, jnp.maximum(i * nblk - 1, 0), 0)
    knext = lambda b, i: (b, jnp.minimum((i + 1) * nblk, nsb - 1), 0)
    vprev = lambda b, i: (b, jnp.maximum(i * nblk - 1, 0), 0, 0)
    vnext = lambda b, i: (b, jnp.minimum((i + 1) * nblk, nsb - 1), 0, 0)
    return pl.pallas_call(
        functools.partial(_attn_kernel, tm=tm, nt=nt),
        grid=(B, nt),
        in_specs=[
            pl.BlockSpec((1, tm, D), main),
            pl.BlockSpec((1, nblk, D, BLOCK), lambda b, i: (b, i, 0, 0)),
            pl.BlockSpec((1, tm, KV_DIM), main),
            pl.BlockSpec((1, BLOCK, KV_DIM), kprev),
            pl.BlockSpec((1, BLOCK, KV_DIM), knext),
            pl.BlockSpec((1, nblk, KV_DIM, BLOCK), lambda b, i: (b, i, 0, 0)),
            pl.BlockSpec((1, 1, KV_DIM, BLOCK), vprev),
            pl.BlockSpec((1, 1, KV_DIM, BLOCK), vnext),
            _resident(bias.shape), _resident(sink.shape), _resident(wo.shape), _resident(ln.shape),
        ],
        out_specs=pl.BlockSpec((1, tm, D), main),
        out_shape=jax.ShapeDtypeStruct(x.shape, F32),
        scratch_shapes=[
            pltpu.VMEM((tm + 2 * BLOCK, KV_DIM), BF16),
            pltpu.VMEM((nblk + 2, KV_DIM, BLOCK), BF16),
            pltpu.VMEM((nblk, D, BLOCK), BF16),
            pltpu.VMEM((3 * BLOCK, N_KV_HEADS * BLOCK), F32),
            pltpu.VMEM((3 * BLOCK, N_KV_HEADS * BLOCK), F32),
            pltpu.VMEM((3 * BLOCK, N_KV_HEADS * BLOCK), BF16),
            pltpu.VMEM((3 * BLOCK, N_KV_HEADS * BLOCK), BF16),
            pltpu.VMEM((1, N_KV_HEADS * BLOCK), F32),
            pltpu.VMEM((1, N_KV_HEADS * BLOCK), F32),
        ],
        compiler_params=_params(),
        name="window_attn",
    )(x, qt, k, k, k, vt, vt, vt, bias, sink, wo, ln)


def _attn_bias():
    slopes = (2.0 ** (-8.0 * np.arange(1, N_HEADS + 1) / N_HEADS)).astype(np.float32)
    slopes = slopes.reshape(N_KV_HEADS, GROUP)
    kpos = np.arange(3 * BLOCK)[:, None] - BLOCK
    qpos = np.arange(BLOCK)[None, :]
    dist = np.abs(qpos - kpos)
    band = dist <= BLOCK
    out = np.empty((3, GROUP, 3 * BLOCK, N_KV_HEADS, BLOCK), np.float32)
    for variant in range(3):
        valid = band
        if variant == 0:
            valid = valid & (kpos >= 0)
        if variant == 2:
            valid = valid & (kpos < BLOCK)
        for r in range(GROUP):
            for g in range(N_KV_HEADS):
                out[variant, r, :, g] = np.where(valid, -slopes[g, r] * dist.astype(np.float32),
                                                 np.float32(NEG_BIG))
    return jnp.asarray(out.reshape(3, GROUP, 3 * BLOCK, N_KV_HEADS * BLOCK))


def _attn_weights(w_q, w_kv, w_o, sink):
    wqt = w_q.reshape(D_MODEL, N_KV_HEADS, GROUP, HEAD_DIM).transpose(2, 1, 3, 0)
    wqt = wqt.reshape(D_MODEL, D_MODEL).astype(BF16)
    wk = w_kv[:, :KV_DIM].astype(BF16)
    wvt = w_kv[:, KV_DIM:].T.astype(BF16)
    wo = w_o.reshape(N_KV_HEADS, GROUP, HEAD_DIM, D_MODEL).transpose(1, 0, 2, 3)
    wo = wo.reshape(D_MODEL, D_MODEL).astype(BF16)
    sk = sink.astype(F32).reshape(N_KV_HEADS, GROUP).T
    sk = jnp.repeat(sk, BLOCK, axis=1)[:, None, :]
    return wqt, wk, wvt, wo, sk


def _trunk(x, layers, tm):
    for kind, mixer, ffn in layers:
        if kind == "conv":
            x = _conv_call(x, *mixer, tm)
        else:
            wqt, wk, wvt, wo, sk, bias, ln = mixer
            qt, k, vt = _qkv_call(x, wqt, wk, wvt, tm)
            x = _attn_call(x, qt, k, vt, bias, sk, wo, ln, tm)
        x = _ffn_call(x, *ffn, tm)
    return x


def kernel(x_prompt, x_sample, conv_w_pw1, conv_b_pw1, conv_w_dw, conv_b_dw, conv_norm_g,
           conv_norm_b, conv_w_pw2, conv_b_pw2, attn_w_q, attn_w_kv, attn_w_o, attn_sink,
           ffn_w_up, ffn_w_dw, ffn_b_dw, ffn_w_down, ln_mix_g, ln_mix_b, ln_ffn_g, ln_ffn_b):
    bias = _attn_bias()
    layers = []
    for i in range(DEPTH):
        j = i // 2
        if i % 2 == 0:
            wa = conv_w_pw1[j][:, :D_MODEL].astype(BF16)
            wgt = conv_w_pw1[j][:, D_MODEL:].astype(BF16)
            dw = jnp.pad(conv_w_dw[j], ((0, 32 - CONV_WIDTH), (0, 0)))
            vec = jnp.stack([conv_b_pw1[j][:D_MODEL], conv_b_pw1[j][D_MODEL:], conv_b_dw[j],
                             conv_norm_g[j], conv_norm_b[j], conv_b_pw2[j],
                             ln_mix_g[i], ln_mix_b[i]], axis=0)
            mixer = (wa, wgt, dw, conv_w_pw2[j].astype(BF16), vec)
            kind = "conv"
        else:
            wqt, wk, wvt, wo, sk = _attn_weights(attn_w_q[j], attn_w_kv[j], attn_w_o[j],
                                                 attn_sink[j])
            ln = jnp.stack([ln_mix_g[i], ln_mix_b[i]], axis=0)
            mixer = (wqt, wk, wvt, wo, sk, bias, ln)
            kind = "attn"
        ffn = _ffn_weights(ffn_w_up[i], ffn_w_dw[i], ffn_b_dw[i], ffn_w_down[i])
        ffn = ffn + (jnp.stack([ln_ffn_g[i], ln_ffn_b[i]], axis=0),)
        layers.append((kind, mixer, ffn))
    return (_trunk(x_prompt, layers, ROW_TILE), _trunk(x_sample, layers, ROW_TILE))
```

```python
import functools

import jax
import jax.numpy as jnp
import numpy as np
from jax import lax
from jax.experimental import pallas as pl
from jax.experimental.pallas import tpu as pltpu

D_MODEL = 1024
DEPTH = 4
CONV_WIDTH = 31
CONV_HALO = 16
N_HEADS = 16
HEAD_DIM = 64
N_KV_HEADS = 4
GROUP = N_HEADS // N_KV_HEADS
KV_DIM = N_KV_HEADS * HEAD_DIM
BLOCK = 128
D_FF = 2752
FFN_CHUNK = 256
D_FF_PAD = 2816
N_FFN_CHUNKS = D_FF_PAD // FFN_CHUNK
FFN_HALO = 8
FFN_OUT_ROWS = 256
LN_EPS = 1e-5
NEG_BIG = -1e30
DEEPNORM_ALPHA = (2.0 * DEPTH) ** 0.25

ROW_TILE = 1024
LANES = 128
N_SLABS = D_MODEL // LANES
CONV_ROWS = 64
CONV_PHASES = 4
OPROJ_ROWS = 256
VMEM_LIMIT = 56 * 1024 * 1024

F32 = jnp.float32
BF16 = jnp.bfloat16
_NT = (((1,), (1,)), ((), ()))
_TN = (((0,), (0,)), ((), ()))


def _layer_norm(y, g, b):
    mu = jnp.mean(y, axis=-1, keepdims=True)
    d = y - mu
    var = jnp.mean(d * d, axis=-1, keepdims=True)
    return d * lax.rsqrt(var + LN_EPS) * g + b


def _resident(shape):
    zeros = (0,) * len(shape)
    return pl.BlockSpec(shape, lambda b, i: zeros, pipeline_mode=pl.Buffered(1))


def _params():
    return pltpu.CompilerParams(
        dimension_semantics=("parallel", "arbitrary"),
        vmem_limit_bytes=VMEM_LIMIT)


def _ffn_kernel(xm_ref, xp_ref, xn_ref, wg_ref, wu_ref, cw_ref, wd_ref, ln_ref, o_ref,
                xe_ref, h_ref, *, tm, nt):
    i = pl.program_id(1)
    xm = xm_ref[0]
    prev = jnp.where(i > 0, xp_ref[0], 0.0)
    nxt = jnp.where(i < nt - 1, xn_ref[0], 0.0)
    xe_ref[...] = jnp.concatenate([prev, xm, nxt], axis=0).astype(BF16)
    rows = tm + 2 * FFN_HALO

    def conv3(h, cw, base):
        up = pltpu.roll(h, 1, axis=0)[FFN_HALO:FFN_HALO + tm]
        dn = pltpu.roll(h, rows - 1, axis=0)[FFN_HALO:FFN_HALO + tm]
        mid = h[FFN_HALO:FFN_HALO + tm]
        return (cw[base:base + 1] * up + cw[base + 1:base + 2] * mid
                + cw[base + 2:base + 3] * dn + cw[base + 3:base + 4])

    for j in range(N_FFN_CHUNKS):
        xe = xe_ref[...]
        sl = slice(j * FFN_CHUNK, (j + 1) * FFN_CHUNK)
        cw = cw_ref[:, sl]
        g = conv3(jnp.dot(xe, wg_ref[:, sl], preferred_element_type=F32), cw, 0)
        u = conv3(jnp.dot(xe, wu_ref[:, sl], preferred_element_type=F32), cw, 4)
        act = (g * jax.nn.sigmoid(g)) * u
        h_ref[:, sl] = act.astype(BF16)

    for c in range(tm // FFN_OUT_ROWS):
        rows_c = slice(c * FFN_OUT_ROWS, (c + 1) * FFN_OUT_ROWS)
        f = jnp.dot(h_ref[rows_c, :], wd_ref[...], preferred_element_type=F32)
        o_ref[0, rows_c] = _layer_norm(DEEPNORM_ALPHA * xm_ref[0, rows_c] + f,
                                       ln_ref[0:1], ln_ref[1:2])


def _ffn_call(x, wg, wu, cw, wd, ln, tm):
    B, S, D = x.shape
    nt = S // tm
    hb = tm // FFN_HALO
    nhb = S // FFN_HALO
    return pl.pallas_call(
        functools.partial(_ffn_kernel, tm=tm, nt=nt),
        grid=(B, nt),
        in_specs=[
            pl.BlockSpec((1, tm, D), lambda b, i: (b, i, 0)),
            pl.BlockSpec((1, FFN_HALO, D), lambda b, i: (b, jnp.maximum(i * hb - 1, 0), 0)),
            pl.BlockSpec((1, FFN_HALO, D), lambda b, i: (b, jnp.minimum((i + 1) * hb, nhb - 1), 0)),
            _resident(wg.shape), _resident(wu.shape), _resident(cw.shape),
            _resident(wd.shape), _resident(ln.shape),
        ],
        out_specs=pl.BlockSpec((1, tm, D), lambda b, i: (b, i, 0)),
        out_shape=jax.ShapeDtypeStruct(x.shape, F32),
        scratch_shapes=[
            pltpu.VMEM((tm + 2 * FFN_HALO, D), BF16),
            pltpu.VMEM((tm, D_FF_PAD), BF16),
        ],
        compiler_params=_params(),
        name="conv_ffn",
    )(x, x, x, wg, wu, cw, wd, ln)


def _ffn_weights(w_up, w_dw, b_dw, w_down):
    pad = D_FF_PAD - D_FF
    wg = jnp.pad(w_up[:, :D_FF], ((0, 0), (0, pad))).astype(BF16)
    wu = jnp.pad(w_up[:, D_FF:], ((0, 0), (0, pad))).astype(BF16)
    taps = jnp.concatenate([w_dw[:, :D_FF], b_dw[None, :D_FF],
                            w_dw[:, D_FF:], b_dw[None, D_FF:]], axis=0)
    cw = jnp.pad(taps, ((0, 0), (0, pad)))
    wd = jnp.pad(w_down, ((0, pad), (0, 0))).astype(BF16)
    return wg, wu, cw, wd


def _conv_kernel(xm_ref, xp_ref, xn_ref, wa_ref, wgt_ref, dw_ref, w2_ref, vec_ref, o_ref,
                 hg_ref, c_ref, *, tm, nt):
    i = pl.program_id(1)
    xm = xm_ref[0]
    rows = tm + 2 * CONV_HALO
    xe = jnp.concatenate([xp_ref[0], xm, xn_ref[0]], axis=0).astype(BF16)
    for c in range(D_MODEL // 256):
        sl = slice(c * 256, (c + 1) * 256)
        a = jnp.dot(xe, wa_ref[:, sl], preferred_element_type=F32) + vec_ref[0:1, sl]
        gt = jnp.dot(xe, wgt_ref[:, sl], preferred_element_type=F32) + vec_ref[1:2, sl]
        glu = a * jax.nn.sigmoid(gt)
        hg_ref[2 * c] = glu[:, :LANES]
        hg_ref[2 * c + 1] = glu[:, LANES:]

    @pl.when(i == 0)
    def _():
        hg_ref[:, 0:CONV_HALO, :] = jnp.zeros((N_SLABS, CONV_HALO, LANES), F32)

    @pl.when(i == nt - 1)
    def _():
        hg_ref[:, CONV_HALO + tm:rows, :] = jnp.zeros((N_SLABS, CONV_HALO, LANES), F32)

    q = CONV_ROWS // CONV_PHASES

    def row_block(rb, carry):
        r0 = rb * CONV_ROWS
        for c in range(N_SLABS):
            for j in range(CONV_PHASES):
                acc = jnp.broadcast_to(dw_ref[c, CONV_WIDTH:CONV_WIDTH + 1, :], (q, LANES))
                for k in range(CONV_WIDTH):
                    tap = hg_ref[c, pl.ds(r0 + j + k + 1, q, stride=CONV_PHASES), :]
                    acc = acc + dw_ref[c, k:k + 1, :] * tap
                c_ref[c, pl.ds(r0 + j, q, stride=CONV_PHASES), :] = acc
        return carry

    lax.fori_loop(0, tm // CONV_ROWS, row_block, 0)

    conv = jnp.concatenate([c_ref[c] for c in range(N_SLABS)], axis=1)
    y = _layer_norm(conv, vec_ref[3:4], vec_ref[4:5])
    s = (y * jax.nn.sigmoid(y)).astype(BF16)
    mix = jnp.dot(s, w2_ref[...], preferred_element_type=F32) + vec_ref[5:6]
    o_ref[0] = _layer_norm(DEEPNORM_ALPHA * xm + mix, vec_ref[6:7], vec_ref[7:8])


def _conv_call(x, wa, wgt, dw, w2, vec, tm):
    B, S, D = x.shape
    nt = S // tm
    hb = tm // CONV_HALO
    nhb = S // CONV_HALO
    return pl.pallas_call(
        functools.partial(_conv_kernel, tm=tm, nt=nt),
        grid=(B, nt),
        in_specs=[
            pl.BlockSpec((1, tm, D), lambda b, i: (b, i, 0)),
            pl.BlockSpec((1, CONV_HALO, D), lambda b, i: (b, jnp.maximum(i * hb - 1, 0), 0)),
            pl.BlockSpec((1, CONV_HALO, D), lambda b, i: (b, jnp.minimum((i + 1) * hb, nhb - 1), 0)),
            _resident(wa.shape), _resident(wgt.shape), _resident(dw.shape),
            _resident(w2.shape), _resident(vec.shape),
        ],
        out_specs=pl.BlockSpec((1, tm, D), lambda b, i: (b, i, 0)),
        out_shape=jax.ShapeDtypeStruct(x.shape, F32),
        scratch_shapes=[
            pltpu.VMEM((N_SLABS, tm + 2 * CONV_HALO, LANES), F32),
            pltpu.VMEM((N_SLABS, tm, LANES), F32),
        ],
        compiler_params=_params(),
        name="conformer_conv",
    )(x, x, x, wa, wgt, dw, w2, vec)


def _qkv_kernel(x_ref, wqt_ref, wk_ref, wvt_ref, qt_ref, k_ref, vt_ref, *, tm):
    x = x_ref[0].astype(BF16)
    qt = lax.dot_general(wqt_ref[...], x, _NT, preferred_element_type=F32) * (HEAD_DIM ** -0.5)
    qt = qt.astype(BF16)
    vt = lax.dot_general(wvt_ref[...], x, _NT, preferred_element_type=F32).astype(BF16)
    for jb in range(tm // BLOCK):
        qt_ref[0, jb] = qt[:, jb * BLOCK:(jb + 1) * BLOCK]
        vt_ref[0, jb] = vt[:, jb * BLOCK:(jb + 1) * BLOCK]
    k_ref[0] = jnp.dot(x, wk_ref[...], preferred_element_type=F32).astype(BF16)


def _qkv_call(x, wqt, wk, wvt, tm):
    B, S, D = x.shape
    nblk = tm // BLOCK
    return pl.pallas_call(
        functools.partial(_qkv_kernel, tm=tm),
        grid=(B, S // tm),
        in_specs=[
            pl.BlockSpec((1, tm, D), lambda b, i: (b, i, 0)),
            _resident(wqt.shape), _resident(wk.shape), _resident(wvt.shape),
        ],
        out_specs=[
            pl.BlockSpec((1, nblk, D, BLOCK), lambda b, i: (b, i, 0, 0)),
            pl.BlockSpec((1, tm, KV_DIM), lambda b, i: (b, i, 0)),
            pl.BlockSpec((1, nblk, KV_DIM, BLOCK), lambda b, i: (b, i, 0, 0)),
        ],
        out_shape=[
            jax.ShapeDtypeStruct((B, S // BLOCK, D, BLOCK), BF16),
            jax.ShapeDtypeStruct((B, S, KV_DIM), BF16),
            jax.ShapeDtypeStruct((B, S // BLOCK, KV_DIM, BLOCK), BF16),
        ],
        compiler_params=_params(),
        name="qkv_proj",
    )(x, wqt, wk, wvt)


def _attn_kernel(x_ref, qt_ref, km_ref, kp_ref, kn_ref, vm_ref, vp_ref, vn_ref,
                 bias_ref, sink_ref, wo_ref, ln_ref, o_ref,
                 kcat_ref, vblk_ref, att_ref, s_ref, p_ref, inv_ref, *, tm, nt):
    i = pl.program_id(1)
    nblk = tm // BLOCK
    last_blk = nt * nblk - 1
    kcat_ref[0:BLOCK] = kp_ref[0]
    kcat_ref[BLOCK:BLOCK + tm] = km_ref[0]
    kcat_ref[BLOCK + tm:2 * BLOCK + tm] = kn_ref[0]
    vblk_ref[0] = vp_ref[0, 0]
    for jb in range(nblk):
        vblk_ref[jb + 1] = vm_ref[0, jb]
    vblk_ref[nblk + 1] = vn_ref[0, 0]

    def stage_a(t):
        jb, r = divmod(t, GROUP)
        row0 = jb * BLOCK
        qt = qt_ref[0, jb, r * KV_DIM:(r + 1) * KV_DIM, :]
        zero = jnp.zeros((HEAD_DIM, BLOCK), BF16)
        qs = jnp.concatenate(
            [jnp.concatenate([qt[g * HEAD_DIM:(g + 1) * HEAD_DIM] if h == g else zero
                              for h in range(N_KV_HEADS)], axis=0)
             for g in range(N_KV_HEADS)], axis=1)
        kband = kcat_ref[row0:row0 + 3 * BLOCK, :]
        s_ref[...] = jnp.dot(kband, qs, preferred_element_type=F32)

    def stage_b(t):
        jb, r = divmod(t, GROUP)
        blk = i * nblk + jb
        variant = jnp.where(blk == 0, 0, jnp.where(blk == last_blk, 2, 1))
        for g in range(N_KV_HEADS):
            cs = slice(g * BLOCK, (g + 1) * BLOCK)
            s = s_ref[:, cs] + bias_ref[variant, r, :, cs]
            sink = sink_ref[r, :, cs]
            m = jnp.maximum(jnp.max(s, axis=0, keepdims=True), sink)
            p = jnp.exp(s - m)
            inv_ref[:, cs] = 1.0 / (jnp.sum(p, axis=0, keepdims=True) + jnp.exp(sink - m))
            p_ref[:, cs] = p.astype(BF16)

    def stage_c(t):
        jb, r = divmod(t, GROUP)
        vband = jnp.concatenate([vblk_ref[jb], vblk_ref[jb + 1], vblk_ref[jb + 2]], axis=1)
        o = jnp.dot(vband, p_ref[...], preferred_element_type=F32)
        inv = inv_ref[...]
        out = jnp.concatenate(
            [o[g * HEAD_DIM:(g + 1) * HEAD_DIM, g * BLOCK:(g + 1) * BLOCK]
             * inv[:, g * BLOCK:(g + 1) * BLOCK] for g in range(N_KV_HEADS)], axis=0)
        att_ref[jb, r * KV_DIM:(r + 1) * KV_DIM, :] = out.astype(BF16)

    nitems = nblk * GROUP
    blocks_per_chunk = OPROJ_ROWS // BLOCK

    def out_chunk(c):
        rows = slice(c * OPROJ_ROWS, (c + 1) * OPROJ_ROWS)
        att = jnp.concatenate([att_ref[c * blocks_per_chunk + j]
                               for j in range(blocks_per_chunk)], axis=1)
        mix = lax.dot_general(att, wo_ref[...], _TN, preferred_element_type=F32)
        o_ref[0, rows] = _layer_norm(DEEPNORM_ALPHA * x_ref[0, rows] + mix,
                                     ln_ref[0:1], ln_ref[1:2])

    for t in range(-1, nitems + 1):
        if 0 <= t - 1 < nitems:
            stage_c(t - 1)
        if 0 <= t < nitems:
            stage_b(t)
        if 0 <= t + 1 < nitems:
            stage_a(t + 1)
    for c in range(tm // OPROJ_ROWS):
        out_chunk(c)


def _attn_call(x, qt, k, vt, bias, sink, wo, ln, tm):
    B, S, D = x.shape
    nt = S // tm
    nblk = tm // BLOCK
    nsb = S // BLOCK
    main = lambda b, i: (b, i, 0)
    kprev = lambda b, i: (b, jnp.maximum(i * nblk - 1, 0), 0)
    knext = lambda b, i: (b, jnp.minimum((i + 1) * nblk, nsb - 1), 0)
    vprev = lambda b, i: (b, jnp.maximum(i * nblk - 1, 0), 0, 0)
    vnext = lambda b, i: (b, jnp.minimum((i + 1) * nblk, nsb - 1), 0, 0)
    return pl.pallas_call(
        functools.partial(_attn_kernel, tm=tm, nt=nt),
        grid=(B, nt),
        in_specs=[
            pl.BlockSpec((1, tm, D), main),
            pl.BlockSpec((1, nblk, D, BLOCK), lambda b, i: (b, i, 0, 0)),
            pl.BlockSpec((1, tm, KV_DIM), main),
            pl.BlockSpec((1, BLOCK, KV_DIM), kprev),
            pl.BlockSpec((1, BLOCK, KV_DIM), knext),
            pl.BlockSpec((1, nblk, KV_DIM, BLOCK), lambda b, i: (b, i, 0, 0)),
            pl.BlockSpec((1, 1, KV_DIM, BLOCK), vprev),
            pl.BlockSpec((1, 1, KV_DIM, BLOCK), vnext),
            _resident(bias.shape), _resident(sink.shape), _resident(wo.shape), _resident(ln.shape),
        ],
        out_specs=pl.BlockSpec((1, tm, D), main),
        out_shape=jax.ShapeDtypeStruct(x.shape, F32),
        scratch_shapes=[
            pltpu.VMEM((tm + 2 * BLOCK, KV_DIM), BF16),
            pltpu.VMEM((nblk + 2, KV_DIM, BLOCK), BF16),
            pltpu.VMEM((nblk, D, BLOCK), BF16),
            pltpu.VMEM((3 * BLOCK, N_KV_HEADS * BLOCK), F32),
            pltpu.VMEM((3 * BLOCK, N_KV_HEADS * BLOCK), BF16),
            pltpu.VMEM((1, N_KV_HEADS * BLOCK), F32),
        ],
        compiler_params=_params(),
        name="window_attn",
    )(x, qt, k, k, k, vt, vt, vt, bias, sink, wo, ln)


def _attn_bias():
    slopes = (2.0 ** (-8.0 * np.arange(1, N_HEADS + 1) / N_HEADS)).astype(np.float32)
    slopes = slopes.reshape(N_KV_HEADS, GROUP)
    kpos = np.arange(3 * BLOCK)[:, None] - BLOCK
    qpos = np.arange(BLOCK)[None, :]
    dist = np.abs(qpos - kpos)
    band = dist <= BLOCK
    out = np.empty((3, GROUP, 3 * BLOCK, N_KV_HEADS, BLOCK), np.float32)
    for variant in range(3):
        valid = band
        if variant == 0:
            valid = valid & (kpos >= 0)
        if variant == 2:
            valid = valid & (kpos < BLOCK)
        for r in range(GROUP):
            for g in range(N_KV_HEADS):
                out[variant, r, :, g] = np.where(valid, -slopes[g, r] * dist.astype(np.float32),
                                                 np.float32(NEG_BIG))
    return jnp.asarray(out.reshape(3, GROUP, 3 * BLOCK, N_KV_HEADS * BLOCK))


def _attn_weights(w_q, w_kv, w_o, sink):
    wqt = w_q.reshape(D_MODEL, N_KV_HEADS, GROUP, HEAD_DIM).transpose(2, 1, 3, 0)
    wqt = wqt.reshape(D_MODEL, D_MODEL).astype(BF16)
    wk = w_kv[:, :KV_DIM].astype(BF16)
    wvt = w_kv[:, KV_DIM:].T.astype(BF16)
    wo = w_o.reshape(N_KV_HEADS, GROUP, HEAD_DIM, D_MODEL).transpose(1, 0, 2, 3)
    wo = wo.reshape(D_MODEL, D_MODEL).astype(BF16)
    sk = sink.astype(F32).reshape(N_KV_HEADS, GROUP).T
    sk = jnp.repeat(sk, BLOCK, axis=1)[:, None, :]
    return wqt, wk, wvt, wo, sk


def _trunk(x, layers, tm):
    for kind, mixer, ffn in layers:
        if kind == "conv":
            x = _conv_call(x, *mixer, tm)
        else:
            wqt, wk, wvt, wo, sk, bias, ln = mixer
            qt, k, vt = _qkv_call(x, wqt, wk, wvt, tm)
            x = _attn_call(x, qt, k, vt, bias, sk, wo, ln, tm)
        x = _ffn_call(x, *ffn, tm)
    return x


def kernel(x_prompt, x_sample, conv_w_pw1, conv_b_pw1, conv_w_dw, conv_b_dw, conv_norm_g,
           conv_norm_b, conv_w_pw2, conv_b_pw2, attn_w_q, attn_w_kv, attn_w_o, attn_sink,
           ffn_w_up, ffn_w_dw, ffn_b_dw, ffn_w_down, ln_mix_g, ln_mix_b, ln_ffn_g, ln_ffn_b):
    bias = _attn_bias()
    layers = []
    for i in range(DEPTH):
        j = i // 2
        if i % 2 == 0:
            wa = conv_w_pw1[j][:, :D_MODEL].astype(BF16)
            wgt = conv_w_pw1[j][:, D_MODEL:].astype(BF16)
            dw = jnp.concatenate([conv_w_dw[j], conv_b_dw[j][None]], axis=0)
            dw = dw.reshape(CONV_WIDTH + 1, N_SLABS, LANES).transpose(1, 0, 2)
            vec = jnp.stack([conv_b_pw1[j][:D_MODEL], conv_b_pw1[j][D_MODEL:], conv_b_dw[j],
                             conv_norm_g[j], conv_norm_b[j], conv_b_pw2[j],
                             ln_mix_g[i], ln_mix_b[i]], axis=0)
            mixer = (wa, wgt, dw, conv_w_pw2[j].astype(BF16), vec)
            kind = "conv"
        else:
            wqt, wk, wvt, wo, sk = _attn_weights(attn_w_q[j], attn_w_kv[j], attn_w_o[j],
                                                 attn_sink[j])
            ln = jnp.stack([ln_mix_g[i], ln_mix_b[i]], axis=0)
            mixer = (wqt, wk, wvt, wo, sk, bias, ln)
            kind = "attn"
        ffn = _ffn_weights(ffn_w_up[i], ffn_w_dw[i], ffn_b_dw[i], ffn_w_down[i])
        ffn = ffn + (jnp.stack([ln_ffn_g[i], ln_ffn_b[i]], axis=0),)
        layers.append((kind, mixer, ffn))
    return (_trunk(x_prompt, layers, ROW_TILE), _trunk(x_sample, layers, ROW_TILE))
```

```python
import functools

import jax
import jax.numpy as jnp
import numpy as np
from jax import lax
from jax.experimental import pallas as pl
from jax.experimental.pallas import tpu as pltpu

D_MODEL = 1024
DEPTH = 4
CONV_WIDTH = 31
CONV_HALO = 16
N_HEADS = 16
HEAD_DIM = 64
N_KV_HEADS = 4
GROUP = N_HEADS // N_KV_HEADS
KV_DIM = N_KV_HEADS * HEAD_DIM
BLOCK = 128
D_FF = 2752
FFN_CHUNK = 256
D_FF_PAD = 2816
N_FFN_CHUNKS = D_FF_PAD // FFN_CHUNK
FFN_HALO = 8
FFN_OUT_ROWS = 256
FFN_PHASES = 4
LN_EPS = 1e-5
NEG_BIG = -1e30
DEEPNORM_ALPHA = (2.0 * DEPTH) ** 0.25

ROW_TILE = 1024
LANES = 128
N_SLABS = D_MODEL // LANES
CONV_ROWS = 64
CONV_PHASES = 4
OPROJ_ROWS = 256
VMEM_LIMIT = 56 * 1024 * 1024

F32 = jnp.float32
BF16 = jnp.bfloat16
_NT = (((1,), (1,)), ((), ()))
_TN = (((0,), (0,)), ((), ()))


def _layer_norm(y, g, b):
    mu = jnp.mean(y, axis=-1, keepdims=True)
    d = y - mu
    var = jnp.mean(d * d, axis=-1, keepdims=True)
    return d * lax.rsqrt(var + LN_EPS) * g + b


def _resident(shape):
    zeros = (0,) * len(shape)
    return pl.BlockSpec(shape, lambda b, i: zeros, pipeline_mode=pl.Buffered(1))


def _params():
    return pltpu.CompilerParams(
        dimension_semantics=("parallel", "arbitrary"),
        vmem_limit_bytes=VMEM_LIMIT)


def _ffn_kernel(xm_ref, xp_ref, xn_ref, wg_ref, wu_ref, cw_ref, wd_ref, ln_ref, o_ref,
                xe_ref, h_ref, gu_ref, act_ref, *, tm, nt):
    i = pl.program_id(1)
    xm = xm_ref[0]
    prev = jnp.where(i > 0, xp_ref[0], 0.0)
    nxt = jnp.where(i < nt - 1, xn_ref[0], 0.0)
    xe_ref[...] = jnp.concatenate([prev, xm, nxt], axis=0).astype(BF16)
    qrows = tm // FFN_PHASES
    slabs = FFN_CHUNK // LANES

    def conv3(par, slab, phase, cw, base):
        taps = [gu_ref[par, slab, pl.ds(FFN_HALO + d + phase, qrows, stride=FFN_PHASES), :]
                for d in (-1, 0, 1)]
        return (cw[base:base + 1] * taps[0] + cw[base + 1:base + 2] * taps[1]
                + cw[base + 2:base + 3] * taps[2] + cw[base + 3:base + 4])

    for j in range(N_FFN_CHUNKS):
        xe = xe_ref[...]
        par = j % 2
        sl = slice(j * FFN_CHUNK, (j + 1) * FFN_CHUNK)
        g_ext = jnp.dot(xe, wg_ref[:, sl], preferred_element_type=F32)
        u_ext = jnp.dot(xe, wu_ref[:, sl], preferred_element_type=F32)
        for hh in range(slabs):
            gu_ref[par, hh] = g_ext[:, hh * LANES:(hh + 1) * LANES]
            gu_ref[par, slabs + hh] = u_ext[:, hh * LANES:(hh + 1) * LANES]
        for hh in range(slabs):
            cw = cw_ref[:, j * FFN_CHUNK + hh * LANES:j * FFN_CHUNK + (hh + 1) * LANES]
            for phase in range(FFN_PHASES):
                g = conv3(par, hh, phase, cw, 0)
                u = conv3(par, slabs + hh, phase, cw, 4)
                act_ref[par, hh, pl.ds(phase, qrows, stride=FFN_PHASES), :] = (
                    (g * jax.nn.sigmoid(g)) * u)
        h_ref[:, sl] = jnp.concatenate([act_ref[par, hh] for hh in range(slabs)],
                                       axis=1).astype(BF16)

    for c in range(tm // FFN_OUT_ROWS):
        rows_c = slice(c * FFN_OUT_ROWS, (c + 1) * FFN_OUT_ROWS)
        f = jnp.dot(h_ref[rows_c, :], wd_ref[...], preferred_element_type=F32)
        o_ref[0, rows_c] = _layer_norm(DEEPNORM_ALPHA * xm_ref[0, rows_c] + f,
                                       ln_ref[0:1], ln_ref[1:2])


def _ffn_call(x, wg, wu, cw, wd, ln, tm):
    B, S, D = x.shape
    nt = S // tm
    hb = tm // FFN_HALO
    nhb = S // FFN_HALO
    return pl.pallas_call(
        functools.partial(_ffn_kernel, tm=tm, nt=nt),
        grid=(B, nt),
        in_specs=[
            pl.BlockSpec((1, tm, D), lambda b, i: (b, i, 0)),
            pl.BlockSpec((1, FFN_HALO, D), lambda b, i: (b, jnp.maximum(i * hb - 1, 0), 0)),
            pl.BlockSpec((1, FFN_HALO, D), lambda b, i: (b, jnp.minimum((i + 1) * hb, nhb - 1), 0)),
            _resident(wg.shape), _resident(wu.shape), _resident(cw.shape),
            _resident(wd.shape), _resident(ln.shape),
        ],
        out_specs=pl.BlockSpec((1, tm, D), lambda b, i: (b, i, 0)),
        out_shape=jax.ShapeDtypeStruct(x.shape, F32),
        scratch_shapes=[
            pltpu.VMEM((tm + 2 * FFN_HALO, D), BF16),
            pltpu.VMEM((tm, D_FF_PAD), BF16),
            pltpu.VMEM((2, 2 * FFN_CHUNK // LANES, tm + 2 * FFN_HALO, LANES), F32),
            pltpu.VMEM((2, FFN_CHUNK // LANES, tm, LANES), F32),
        ],
        compiler_params=_params(),
        name="conv_ffn",
    )(x, x, x, wg, wu, cw, wd, ln)


def _ffn_weights(w_up, w_dw, b_dw, w_down):
    pad = D_FF_PAD - D_FF
    wg = jnp.pad(w_up[:, :D_FF], ((0, 0), (0, pad))).astype(BF16)
    wu = jnp.pad(w_up[:, D_FF:], ((0, 0), (0, pad))).astype(BF16)
    taps = jnp.concatenate([w_dw[:, :D_FF], b_dw[None, :D_FF],
                            w_dw[:, D_FF:], b_dw[None, D_FF:]], axis=0)
    cw = jnp.pad(taps, ((0, 0), (0, pad)))
    wd = jnp.pad(w_down, ((0, pad), (0, 0))).astype(BF16)
    return wg, wu, cw, wd


def _conv_kernel(xm_ref, xp_ref, xn_ref, wa_ref, wgt_ref, dw_ref, w2_ref, vec_ref, o_ref,
                 hg_ref, c_ref, *, tm, nt):
    i = pl.program_id(1)
    xm = xm_ref[0]
    rows = tm + 2 * CONV_HALO
    xe = jnp.concatenate([xp_ref[0], xm, xn_ref[0]], axis=0).astype(BF16)
    for c in range(D_MODEL // 256):
        sl = slice(c * 256, (c + 1) * 256)
        a = jnp.dot(xe, wa_ref[:, sl], preferred_element_type=F32) + vec_ref[0:1, sl]
        gt = jnp.dot(xe, wgt_ref[:, sl], preferred_element_type=F32) + vec_ref[1:2, sl]
        glu = a * jax.nn.sigmoid(gt)
        hg_ref[2 * c] = glu[:, :LANES]
        hg_ref[2 * c + 1] = glu[:, LANES:]

    @pl.when(i == 0)
    def _():
        hg_ref[:, 0:CONV_HALO, :] = jnp.zeros((N_SLABS, CONV_HALO, LANES), F32)

    @pl.when(i == nt - 1)
    def _():
        hg_ref[:, CONV_HALO + tm:rows, :] = jnp.zeros((N_SLABS, CONV_HALO, LANES), F32)

    q = CONV_ROWS // CONV_PHASES

    def row_block(rb, carry):
        r0 = rb * CONV_ROWS
        for c in range(N_SLABS):
            for j in range(CONV_PHASES):
                acc = jnp.broadcast_to(dw_ref[c, CONV_WIDTH:CONV_WIDTH + 1, :], (q, LANES))
                for k in range(CONV_WIDTH):
                    tap = hg_ref[c, pl.ds(r0 + j + k + 1, q, stride=CONV_PHASES), :]
                    acc = acc + dw_ref[c, k:k + 1, :] * tap
                c_ref[c, pl.ds(r0 + j, q, stride=CONV_PHASES), :] = acc
        return carry

    lax.fori_loop(0, tm // CONV_ROWS, row_block, 0)

    conv = jnp.concatenate([c_ref[c] for c in range(N_SLABS)], axis=1)
    y = _layer_norm(conv, vec_ref[3:4], vec_ref[4:5])
    s = (y * jax.nn.sigmoid(y)).astype(BF16)
    mix = jnp.dot(s, w2_ref[...], preferred_element_type=F32) + vec_ref[5:6]
    o_ref[0] = _layer_norm(DEEPNORM_ALPHA * xm + mix, vec_ref[6:7], vec_ref[7:8])


def _conv_call(x, wa, wgt, dw, w2, vec, tm):
    B, S, D = x.shape
    nt = S // tm
    hb = tm // CONV_HALO
    nhb = S // CONV_HALO
    return pl.pallas_call(
        functools.partial(_conv_kernel, tm=tm, nt=nt),
        grid=(B, nt),
        in_specs=[
            pl.BlockSpec((1, tm, D), lambda b, i: (b, i, 0)),
            pl.BlockSpec((1, CONV_HALO, D), lambda b, i: (b, jnp.maximum(i * hb - 1, 0), 0)),
            pl.BlockSpec((1, CONV_HALO, D), lambda b, i: (b, jnp.minimum((i + 1) * hb, nhb - 1), 0)),
            _resident(wa.shape), _resident(wgt.shape), _resident(dw.shape),
            _resident(w2.shape), _resident(vec.shape),
        ],
        out_specs=pl.BlockSpec((1, tm, D), lambda b, i: (b, i, 0)),
        out_shape=jax.ShapeDtypeStruct(x.shape, F32),
        scratch_shapes=[
            pltpu.VMEM((N_SLABS, tm + 2 * CONV_HALO, LANES), F32),
            pltpu.VMEM((N_SLABS, tm, LANES), F32),
        ],
        compiler_params=_params(),
        name="conformer_conv",
    )(x, x, x, wa, wgt, dw, w2, vec)


def _qkv_kernel(x_ref, wqt_ref, wk_ref, wvt_ref, qt_ref, k_ref, vt_ref, *, tm):
    x = x_ref[0].astype(BF16)
    qt = lax.dot_general(wqt_ref[...], x, _NT, preferred_element_type=F32) * (HEAD_DIM ** -0.5)
    qt = qt.astype(BF16)
    vt = lax.dot_general(wvt_ref[...], x, _NT, preferred_element_type=F32).astype(BF16)
    for jb in range(tm // BLOCK):
        qt_ref[0, jb] = qt[:, jb * BLOCK:(jb + 1) * BLOCK]
        vt_ref[0, jb] = vt[:, jb * BLOCK:(jb + 1) * BLOCK]
    k_ref[0] = jnp.dot(x, wk_ref[...], preferred_element_type=F32).astype(BF16)


def _qkv_call(x, wqt, wk, wvt, tm):
    B, S, D = x.shape
    nblk = tm // BLOCK
    return pl.pallas_call(
        functools.partial(_qkv_kernel, tm=tm),
        grid=(B, S // tm),
        in_specs=[
            pl.BlockSpec((1, tm, D), lambda b, i: (b, i, 0)),
            _resident(wqt.shape), _resident(wk.shape), _resident(wvt.shape),
        ],
        out_specs=[
            pl.BlockSpec((1, nblk, D, BLOCK), lambda b, i: (b, i, 0, 0)),
            pl.BlockSpec((1, tm, KV_DIM), lambda b, i: (b, i, 0)),
            pl.BlockSpec((1, nblk, KV_DIM, BLOCK), lambda b, i: (b, i, 0, 0)),
        ],
        out_shape=[
            jax.ShapeDtypeStruct((B, S // BLOCK, D, BLOCK), BF16),
            jax.ShapeDtypeStruct((B, S, KV_DIM), BF16),
            jax.ShapeDtypeStruct((B, S // BLOCK, KV_DIM, BLOCK), BF16),
        ],
        compiler_params=_params(),
        name="qkv_proj",
    )(x, wqt, wk, wvt)


def _attn_kernel(x_ref, qt_ref, km_ref, kp_ref, kn_ref, vm_ref, vp_ref, vn_ref,
                 bias_ref, sink_ref, wo_ref, ln_ref, o_ref,
                 kcat_ref, vblk_ref, att_ref, s_ref, p_ref, inv_ref, *, tm, nt):
    i = pl.program_id(1)
    nblk = tm // BLOCK
    last_blk = nt * nblk - 1
    kcat_ref[0:BLOCK] = kp_ref[0]
    kcat_ref[BLOCK:BLOCK + tm] = km_ref[0]
    kcat_ref[BLOCK + tm:2 * BLOCK + tm] = kn_ref[0]
    vblk_ref[0] = vp_ref[0, 0]
    for jb in range(nblk):
        vblk_ref[jb + 1] = vm_ref[0, jb]
    vblk_ref[nblk + 1] = vn_ref[0, 0]

    def stage_a(t):
        jb, r = divmod(t, GROUP)
        row0 = jb * BLOCK
        qt = qt_ref[0, jb, r * KV_DIM:(r + 1) * KV_DIM, :]
        zero = jnp.zeros((HEAD_DIM, BLOCK), BF16)
        qs = jnp.concatenate(
            [jnp.concatenate([qt[g * HEAD_DIM:(g + 1) * HEAD_DIM] if h == g else zero
                              for h in range(N_KV_HEADS)], axis=0)
             for g in range(N_KV_HEADS)], axis=1)
        kband = kcat_ref[row0:row0 + 3 * BLOCK, :]
        s = jnp.dot(kband, qs, preferred_element_type=F32)
        for g in range(N_KV_HEADS):
            s_ref[g] = s[:, g * BLOCK:(g + 1) * BLOCK]

    def stage_b(t):
        jb, r = divmod(t, GROUP)
        blk = i * nblk + jb
        variant = jnp.where(blk == 0, 0, jnp.where(blk == last_blk, 2, 1))
        for g in range(N_KV_HEADS):
            cs = slice(g * BLOCK, (g + 1) * BLOCK)
            s = s_ref[g] + bias_ref[variant, r, g]
            sink = sink_ref[r, :, cs]
            m = jnp.maximum(jnp.max(s, axis=0, keepdims=True), sink)
            p = jnp.exp(s - m)
            inv_ref[:, cs] = 1.0 / (jnp.sum(p, axis=0, keepdims=True) + jnp.exp(sink - m))
            p_ref[g] = p.astype(BF16)

    def stage_c(t):
        jb, r = divmod(t, GROUP)
        vband = jnp.concatenate([vblk_ref[jb], vblk_ref[jb + 1], vblk_ref[jb + 2]], axis=1)
        pt = jnp.concatenate([p_ref[g] for g in range(N_KV_HEADS)], axis=1)
        o = jnp.dot(vband, pt, preferred_element_type=F32)
        inv = inv_ref[...]
        out = jnp.concatenate(
            [o[g * HEAD_DIM:(g + 1) * HEAD_DIM, g * BLOCK:(g + 1) * BLOCK]
             * inv[:, g * BLOCK:(g + 1) * BLOCK] for g in range(N_KV_HEADS)], axis=0)
        att_ref[jb, r * KV_DIM:(r + 1) * KV_DIM, :] = out.astype(BF16)

    nitems = nblk * GROUP
    blocks_per_chunk = OPROJ_ROWS // BLOCK

    def out_chunk(c):
        rows = slice(c * OPROJ_ROWS, (c + 1) * OPROJ_ROWS)
        att = jnp.concatenate([att_ref[c * blocks_per_chunk + j]
                               for j in range(blocks_per_chunk)], axis=1)
        mix = lax.dot_general(att, wo_ref[...], _TN, preferred_element_type=F32)
        o_ref[0, rows] = _layer_norm(DEEPNORM_ALPHA * x_ref[0, rows] + mix,
                                     ln_ref[0:1], ln_ref[1:2])

    for t in range(-1, nitems + 1):
        if 0 <= t - 1 < nitems:
            stage_c(t - 1)
        if 0 <= t < nitems:
            stage_b(t)
        if 0 <= t + 1 < nitems:
            stage_a(t + 1)
    for c in range(tm // OPROJ_ROWS):
        out_chunk(c)


def _attn_call(x, qt, k, vt, bias, sink, wo, ln, tm):
    B, S, D = x.shape
    nt = S // tm
    nblk = tm // BLOCK
    nsb = S // BLOCK
    main = lambda b, i: (b, i, 0)
    kprev = lambda b, i: (b, jnp.maximum(i * nblk - 1, 0), 0)
    knext = lambda b, i: (b, jnp.minimum((i + 1) * nblk, nsb - 1), 0)
    vprev = lambda b, i: (b, jnp.maximum(i * nblk - 1, 0), 0, 0)
    vnext = lambda b, i: (b, jnp.minimum((i + 1) * nblk, nsb - 1), 0, 0)
    return pl.pallas_call(
        functools.partial(_attn_kernel, tm=tm, nt=nt),
        grid=(B, nt),
        in_specs=[
            pl.BlockSpec((1, tm, D), main),
            pl.BlockSpec((1, nblk, D, BLOCK), lambda b, i: (b, i, 0, 0)),
            pl.BlockSpec((1, tm, KV_DIM), main),
            pl.BlockSpec((1, BLOCK, KV_DIM), kprev),
            pl.BlockSpec((1, BLOCK, KV_DIM), knext),
            pl.BlockSpec((1, nblk, KV_DIM, BLOCK), lambda b, i: (b, i, 0, 0)),
            pl.BlockSpec((1, 1, KV_DIM, BLOCK), vprev),
            pl.BlockSpec((1, 1, KV_DIM, BLOCK), vnext),
            _resident(bias.shape), _resident(sink.shape), _resident(wo.shape), _resident(ln.shape),
        ],
        out_specs=pl.BlockSpec((1, tm, D), main),
        out_shape=jax.ShapeDtypeStruct(x.shape, F32),
        scratch_shapes=[
            pltpu.VMEM((tm + 2 * BLOCK, KV_DIM), BF16),
            pltpu.VMEM((nblk + 2, KV_DIM, BLOCK), BF16),
            pltpu.VMEM((nblk, D, BLOCK), BF16),
            pltpu.VMEM((N_KV_HEADS, 3 * BLOCK, BLOCK), F32),
            pltpu.VMEM((N_KV_HEADS, 3 * BLOCK, BLOCK), BF16),
            pltpu.VMEM((1, N_KV_HEADS * BLOCK), F32),
        ],
        compiler_params=_params(),
        name="window_attn",
    )(x, qt, k, k, k, vt, vt, vt, bias, sink, wo, ln)


def _attn_bias():
    slopes = (2.0 ** (-8.0 * np.arange(1, N_HEADS + 1) / N_HEADS)).astype(np.float32)
    slopes = slopes.reshape(N_KV_HEADS, GROUP)
    kpos = np.arange(3 * BLOCK)[:, None] - BLOCK
    qpos = np.arange(BLOCK)[None, :]
    dist = np.abs(qpos - kpos)
    band = dist <= BLOCK
    out = np.empty((3, GROUP, N_KV_HEADS, 3 * BLOCK, BLOCK), np.float32)
    for variant in range(3):
        valid = band
        if variant == 0:
            valid = valid & (kpos >= 0)
        if variant == 2:
            valid = valid & (kpos < BLOCK)
        for r in range(GROUP):
            for g in range(N_KV_HEADS):
                out[variant, r, g] = np.where(valid, -slopes[g, r] * dist.astype(np.float32),
                                              np.float32(NEG_BIG))
    return jnp.asarray(out)


def _attn_weights(w_q, w_kv, w_o, sink):
    wqt = w_q.reshape(D_MODEL, N_KV_HEADS, GROUP, HEAD_DIM).transpose(2, 1, 3, 0)
    wqt = wqt.reshape(D_MODEL, D_MODEL).astype(BF16)
    wk = w_kv[:, :KV_DIM].astype(BF16)
    wvt = w_kv[:, KV_DIM:].T.astype(BF16)
    wo = w_o.reshape(N_KV_HEADS, GROUP, HEAD_DIM, D_MODEL).transpose(1, 0, 2, 3)
    wo = wo.reshape(D_MODEL, D_MODEL).astype(BF16)
    sk = sink.astype(F32).reshape(N_KV_HEADS, GROUP).T
    sk = jnp.repeat(sk, BLOCK, axis=1)[:, None, :]
    return wqt, wk, wvt, wo, sk


def _trunk(x, layers, tm):
    for kind, mixer, ffn in layers:
        if kind == "conv":
            x = _conv_call(x, *mixer, tm)
        else:
            wqt, wk, wvt, wo, sk, bias, ln = mixer
            qt, k, vt = _qkv_call(x, wqt, wk, wvt, tm)
            x = _attn_call(x, qt, k, vt, bias, sk, wo, ln, tm)
        x = _ffn_call(x, *ffn, tm)
    return x


def kernel(x_prompt, x_sample, conv_w_pw1, conv_b_pw1, conv_w_dw, conv_b_dw, conv_norm_g,
           conv_norm_b, conv_w_pw2, conv_b_pw2, attn_w_q, attn_w_kv, attn_w_o, attn_sink,
           ffn_w_up, ffn_w_dw, ffn_b_dw, ffn_w_down, ln_mix_g, ln_mix_b, ln_ffn_g, ln_ffn_b):
    bias = _attn_bias()
    layers = []
    for i in range(DEPTH):
        j = i // 2
        if i % 2 == 0:
            wa = conv_w_pw1[j][:, :D_MODEL].astype(BF16)
            wgt = conv_w_pw1[j][:, D_MODEL:].astype(BF16)
            dw = jnp.concatenate([conv_w_dw[j], conv_b_dw[j][None]], axis=0)
            dw = dw.reshape(CONV_WIDTH + 1, N_SLABS, LANES).transpose(1, 0, 2)
            vec = jnp.stack([conv_b_pw1[j][:D_MODEL], conv_b_pw1[j][D_MODEL:], conv_b_dw[j],
                             conv_norm_g[j], conv_norm_b[j], conv_b_pw2[j],
                             ln_mix_g[i], ln_mix_b[i]], axis=0)
            mixer = (wa, wgt, dw, conv_w_pw2[j].astype(BF16), vec)
            kind = "conv"
        else:
            wqt, wk, wvt, wo, sk = _attn_weights(attn_w_q[j], attn_w_kv[j], attn_w_o[j],
                                                 attn_sink[j])
            ln = jnp.stack([ln_mix_g[i], ln_mix_b[i]], axis=0)
            mixer = (wqt, wk, wvt, wo, sk, bias, ln)
            kind = "attn"
        ffn = _ffn_weights(ffn_w_up[i], ffn_w_dw[i], ffn_b_dw[i], ffn_w_down[i])
        ffn = ffn + (jnp.stack([ln_ffn_g[i], ln_ffn_b[i]], axis=0),)
        layers.append((kind, mixer, ffn))
    return (_trunk(x_prompt, layers, ROW_TILE), _trunk(x_sample, layers, ROW_TILE))
```

```python
import functools

import jax
import jax.numpy as jnp
import numpy as np
from jax import lax
from jax.experimental import pallas as pl
from jax.experimental.pallas import tpu as pltpu

D_MODEL = 1024
DEPTH = 4
CONV_WIDTH = 31
CONV_HALO = 16
N_HEADS = 16
HEAD_DIM = 64
N_KV_HEADS = 4
GROUP = N_HEADS // N_KV_HEADS
KV_DIM = N_KV_HEADS * HEAD_DIM
BLOCK = 128
D_FF = 2752
FFN_CHUNK = 256
D_FF_PAD = 2816
N_FFN_CHUNKS = D_FF_PAD // FFN_CHUNK
FFN_HALO = 8
FFN_OUT_ROWS = 256
FFN_PHASES = 4
LN_EPS = 1e-5
NEG_BIG = -1e30
DEEPNORM_ALPHA = (2.0 * DEPTH) ** 0.25

ROW_TILE = 1024
LANES = 128
N_SLABS = D_MODEL // LANES
CONV_ROWS = 64
CONV_PHASES = 4
CONV_OUT_ROWS = 512
OPROJ_ROWS = 256
VMEM_LIMIT = 56 * 1024 * 1024

F32 = jnp.float32
BF16 = jnp.bfloat16
_NT = (((1,), (1,)), ((), ()))
_TN = (((0,), (0,)), ((), ()))


def _layer_norm(y, g, b):
    mu = jnp.mean(y, axis=-1, keepdims=True)
    d = y - mu
    var = jnp.mean(d * d, axis=-1, keepdims=True)
    return d * lax.rsqrt(var + LN_EPS) * g + b


def _resident(shape):
    zeros = (0,) * len(shape)
    return pl.BlockSpec(shape, lambda b, i: zeros, pipeline_mode=pl.Buffered(1))


def _params():
    return pltpu.CompilerParams(
        dimension_semantics=("parallel", "arbitrary"),
        vmem_limit_bytes=VMEM_LIMIT)


def _ffn_kernel(xm_ref, xp_ref, xn_ref, wg_ref, wu_ref, cw_ref, wd_ref, ln_ref, o_ref,
                xe_ref, h_ref, gu_ref, act_ref, *, tm, nt):
    i = pl.program_id(1)
    xm = xm_ref[0]
    prev = jnp.where(i > 0, xp_ref[0], 0.0)
    nxt = jnp.where(i < nt - 1, xn_ref[0], 0.0)
    xe_ref[...] = jnp.concatenate([prev, xm, nxt], axis=0).astype(BF16)
    qrows = tm // FFN_PHASES
    slabs = FFN_CHUNK // LANES

    def conv3(par, slab, phase, cw, base):
        taps = [gu_ref[par, slab, pl.ds(FFN_HALO + d + phase, qrows, stride=FFN_PHASES), :]
                for d in (-1, 0, 1)]
        return (cw[base:base + 1] * taps[0] + cw[base + 1:base + 2] * taps[1]
                + cw[base + 2:base + 3] * taps[2] + cw[base + 3:base + 4])

    for j in range(N_FFN_CHUNKS):
        xe = xe_ref[...]
        par = j % 2
        sl = slice(j * FFN_CHUNK, (j + 1) * FFN_CHUNK)
        g_ext = jnp.dot(xe, wg_ref[:, sl], preferred_element_type=F32)
        u_ext = jnp.dot(xe, wu_ref[:, sl], preferred_element_type=F32)
        for hh in range(slabs):
            gu_ref[par, hh] = g_ext[:, hh * LANES:(hh + 1) * LANES]
            gu_ref[par, slabs + hh] = u_ext[:, hh * LANES:(hh + 1) * LANES]
        for hh in range(slabs):
            cw = cw_ref[:, j * FFN_CHUNK + hh * LANES:j * FFN_CHUNK + (hh + 1) * LANES]
            for phase in range(FFN_PHASES):
                g = conv3(par, hh, phase, cw, 0)
                u = conv3(par, slabs + hh, phase, cw, 4)
                act_ref[par, hh, pl.ds(phase, qrows, stride=FFN_PHASES), :] = (
                    (g * jax.nn.sigmoid(g)) * u)
        h_ref[:, sl] = jnp.concatenate([act_ref[par, hh] for hh in range(slabs)],
                                       axis=1).astype(BF16)

    for c in range(tm // FFN_OUT_ROWS):
        rows_c = slice(c * FFN_OUT_ROWS, (c + 1) * FFN_OUT_ROWS)
        f = jnp.dot(h_ref[rows_c, :], wd_ref[...], preferred_element_type=F32)
        o_ref[0, rows_c] = _layer_norm(DEEPNORM_ALPHA * xm_ref[0, rows_c] + f,
                                       ln_ref[0:1], ln_ref[1:2])


def _ffn_call(x, wg, wu, cw, wd, ln, tm):
    B, S, D = x.shape
    nt = S // tm
    hb = tm // FFN_HALO
    nhb = S // FFN_HALO
    return pl.pallas_call(
        functools.partial(_ffn_kernel, tm=tm, nt=nt),
        grid=(B, nt),
        in_specs=[
            pl.BlockSpec((1, tm, D), lambda b, i: (b, i, 0)),
            pl.BlockSpec((1, FFN_HALO, D), lambda b, i: (b, jnp.maximum(i * hb - 1, 0), 0)),
            pl.BlockSpec((1, FFN_HALO, D), lambda b, i: (b, jnp.minimum((i + 1) * hb, nhb - 1), 0)),
            _resident(wg.shape), _resident(wu.shape), _resident(cw.shape),
            _resident(wd.shape), _resident(ln.shape),
        ],
        out_specs=pl.BlockSpec((1, tm, D), lambda b, i: (b, i, 0)),
        out_shape=jax.ShapeDtypeStruct(x.shape, F32),
        scratch_shapes=[
            pltpu.VMEM((tm + 2 * FFN_HALO, D), BF16),
            pltpu.VMEM((tm, D_FF_PAD), BF16),
            pltpu.VMEM((2, 2 * FFN_CHUNK // LANES, tm + 2 * FFN_HALO, LANES), F32),
            pltpu.VMEM((2, FFN_CHUNK // LANES, tm, LANES), F32),
        ],
        compiler_params=_params(),
        name="conv_ffn",
    )(x, x, x, wg, wu, cw, wd, ln)


def _ffn_weights(w_up, w_dw, b_dw, w_down):
    pad = D_FF_PAD - D_FF
    wg = jnp.pad(w_up[:, :D_FF], ((0, 0), (0, pad))).astype(BF16)
    wu = jnp.pad(w_up[:, D_FF:], ((0, 0), (0, pad))).astype(BF16)
    taps = jnp.concatenate([w_dw[:, :D_FF], b_dw[None, :D_FF],
                            w_dw[:, D_FF:], b_dw[None, D_FF:]], axis=0)
    cw = jnp.pad(taps, ((0, 0), (0, pad)))
    wd = jnp.pad(w_down, ((0, pad), (0, 0))).astype(BF16)
    return wg, wu, cw, wd


def _conv_kernel(xm_ref, xp_ref, xn_ref, wa_ref, wgt_ref, dw_ref, w2_ref, vec_ref, o_ref,
                 hg_ref, c_ref, *, tm, nt):
    i = pl.program_id(1)
    xm = xm_ref[0]
    rows = tm + 2 * CONV_HALO
    xe = jnp.concatenate([xp_ref[0], xm, xn_ref[0]], axis=0).astype(BF16)
    for c in range(D_MODEL // 256):
        sl = slice(c * 256, (c + 1) * 256)
        a = jnp.dot(xe, wa_ref[:, sl], preferred_element_type=F32) + vec_ref[0:1, sl]
        gt = jnp.dot(xe, wgt_ref[:, sl], preferred_element_type=F32) + vec_ref[1:2, sl]
        glu = a * jax.nn.sigmoid(gt)
        hg_ref[2 * c] = glu[:, :LANES]
        hg_ref[2 * c + 1] = glu[:, LANES:]

    @pl.when(i == 0)
    def _():
        hg_ref[:, 0:CONV_HALO, :] = jnp.zeros((N_SLABS, CONV_HALO, LANES), F32)

    @pl.when(i == nt - 1)
    def _():
        hg_ref[:, CONV_HALO + tm:rows, :] = jnp.zeros((N_SLABS, CONV_HALO, LANES), F32)

    q = CONV_ROWS // CONV_PHASES

    def row_block(rb, carry):
        r0 = rb * CONV_ROWS
        for c in range(N_SLABS):
            for j in range(CONV_PHASES):
                acc = jnp.broadcast_to(dw_ref[c, CONV_WIDTH:CONV_WIDTH + 1, :], (q, LANES))
                for k in range(CONV_WIDTH):
                    tap = hg_ref[c, pl.ds(r0 + j + k + 1, q, stride=CONV_PHASES), :]
                    acc = acc + dw_ref[c, k:k + 1, :] * tap
                c_ref[c, pl.ds(r0 + j, q, stride=CONV_PHASES), :] = acc
        return carry

    lax.fori_loop(0, tm // CONV_ROWS, row_block, 0)

    for r0 in range(0, tm, CONV_OUT_ROWS):
        rs = slice(r0, r0 + CONV_OUT_ROWS)
        conv = jnp.concatenate([c_ref[c, rs, :] for c in range(N_SLABS)], axis=1)
        y = _layer_norm(conv, vec_ref[3:4], vec_ref[4:5])
        s = (y * jax.nn.sigmoid(y)).astype(BF16)
        mix = jnp.dot(s, w2_ref[...], preferred_element_type=F32) + vec_ref[5:6]
        o_ref[0, rs] = _layer_norm(DEEPNORM_ALPHA * xm_ref[0, rs] + mix,
                                   vec_ref[6:7], vec_ref[7:8])


def _conv_call(x, wa, wgt, dw, w2, vec, tm):
    B, S, D = x.shape
    nt = S // tm
    hb = tm // CONV_HALO
    nhb = S // CONV_HALO
    return pl.pallas_call(
        functools.partial(_conv_kernel, tm=tm, nt=nt),
        grid=(B, nt),
        in_specs=[
            pl.BlockSpec((1, tm, D), lambda b, i: (b, i, 0)),
            pl.BlockSpec((1, CONV_HALO, D), lambda b, i: (b, jnp.maximum(i * hb - 1, 0), 0)),
            pl.BlockSpec((1, CONV_HALO, D), lambda b, i: (b, jnp.minimum((i + 1) * hb, nhb - 1), 0)),
            _resident(wa.shape), _resident(wgt.shape), _resident(dw.shape),
            _resident(w2.shape), _resident(vec.shape),
        ],
        out_specs=pl.BlockSpec((1, tm, D), lambda b, i: (b, i, 0)),
        out_shape=jax.ShapeDtypeStruct(x.shape, F32),
        scratch_shapes=[
            pltpu.VMEM((N_SLABS, tm + 2 * CONV_HALO, LANES), F32),
            pltpu.VMEM((N_SLABS, tm, LANES), F32),
        ],
        compiler_params=_params(),
        name="conformer_conv",
    )(x, x, x, wa, wgt, dw, w2, vec)


def _qkv_kernel(x_ref, wqt_ref, wk_ref, wvt_ref, qt_ref, k_ref, vt_ref, *, tm):
    x = x_ref[0].astype(BF16)
    qt = lax.dot_general(wqt_ref[...], x, _NT, preferred_element_type=F32) * (HEAD_DIM ** -0.5)
    qt = qt.astype(BF16)
    vt = lax.dot_general(wvt_ref[...], x, _NT, preferred_element_type=F32).astype(BF16)
    for jb in range(tm // BLOCK):
        qt_ref[0, jb] = qt[:, jb * BLOCK:(jb + 1) * BLOCK]
        vt_ref[0, jb] = vt[:, jb * BLOCK:(jb + 1) * BLOCK]
    k_ref[0] = jnp.dot(x, wk_ref[...], preferred_element_type=F32).astype(BF16)


def _qkv_call(x, wqt, wk, wvt, tm):
    B, S, D = x.shape
    nblk = tm // BLOCK
    return pl.pallas_call(
        functools.partial(_qkv_kernel, tm=tm),
        grid=(B, S // tm),
        in_specs=[
            pl.BlockSpec((1, tm, D), lambda b, i: (b, i, 0)),
            _resident(wqt.shape), _resident(wk.shape), _resident(wvt.shape),
        ],
        out_specs=[
            pl.BlockSpec((1, nblk, D, BLOCK), lambda b, i: (b, i, 0, 0)),
            pl.BlockSpec((1, tm, KV_DIM), lambda b, i: (b, i, 0)),
            pl.BlockSpec((1, nblk, KV_DIM, BLOCK), lambda b, i: (b, i, 0, 0)),
        ],
        out_shape=[
            jax.ShapeDtypeStruct((B, S // BLOCK, D, BLOCK), BF16),
            jax.ShapeDtypeStruct((B, S, KV_DIM), BF16),
            jax.ShapeDtypeStruct((B, S // BLOCK, KV_DIM, BLOCK), BF16),
        ],
        compiler_params=_params(),
        name="qkv_proj",
    )(x, wqt, wk, wvt)


def _attn_kernel(x_ref, qt_ref, km_ref, kp_ref, kn_ref, vm_ref, vp_ref, vn_ref,
                 bias_ref, sink_ref, wo_ref, ln_ref, o_ref,
                 kcat_ref, vblk_ref, att_ref, s_ref, p_ref, inv_ref, *, tm, nt):
    i = pl.program_id(1)
    nblk = tm // BLOCK
    last_blk = nt * nblk - 1
    kcat_ref[0:BLOCK] = kp_ref[0]
    kcat_ref[BLOCK:BLOCK + tm] = km_ref[0]
    kcat_ref[BLOCK + tm:2 * BLOCK + tm] = kn_ref[0]
    vblk_ref[0] = vp_ref[0, 0]
    for jb in range(nblk):
        vblk_ref[jb + 1] = vm_ref[0, jb]
    vblk_ref[nblk + 1] = vn_ref[0, 0]

    def stage_a(t):
        jb, r = divmod(t, GROUP)
        row0 = jb * BLOCK
        qt = qt_ref[0, jb, r * KV_DIM:(r + 1) * KV_DIM, :]
        zero = jnp.zeros((HEAD_DIM, BLOCK), BF16)
        qs = jnp.concatenate(
            [jnp.concatenate([qt[g * HEAD_DIM:(g + 1) * HEAD_DIM] if h == g else zero
                              for h in range(N_KV_HEADS)], axis=0)
             for g in range(N_KV_HEADS)], axis=1)
        kband = kcat_ref[row0:row0 + 3 * BLOCK, :]
        s = jnp.dot(kband, qs, preferred_element_type=F32)
        for g in range(N_KV_HEADS):
            s_ref[g] = s[:, g * BLOCK:(g + 1) * BLOCK]

    def stage_b(t):
        jb, r = divmod(t, GROUP)
        blk = i * nblk + jb
        variant = jnp.where(blk == 0, 0, jnp.where(blk == last_blk, 2, 1))
        for g in range(N_KV_HEADS):
            cs = slice(g * BLOCK, (g + 1) * BLOCK)
            s = s_ref[g] + bias_ref[variant, r, g]
            sink = sink_ref[r, :, cs]
            m = jnp.maximum(jnp.max(s, axis=0, keepdims=True), sink)
            p = jnp.exp(s - m)
            inv_ref[:, cs] = 1.0 / (jnp.sum(p, axis=0, keepdims=True) + jnp.exp(sink - m))
            p_ref[g] = p.astype(BF16)

    def stage_c(t):
        jb, r = divmod(t, GROUP)
        vband = jnp.concatenate([vblk_ref[jb], vblk_ref[jb + 1], vblk_ref[jb + 2]], axis=1)
        inv = inv_ref[...]
        out = jnp.concatenate(
            [jnp.dot(vband[g * HEAD_DIM:(g + 1) * HEAD_DIM], p_ref[g],
                     preferred_element_type=F32) * inv[:, g * BLOCK:(g + 1) * BLOCK]
             for g in range(N_KV_HEADS)], axis=0)
        att_ref[jb, r * KV_DIM:(r + 1) * KV_DIM, :] = out.astype(BF16)

    nitems = nblk * GROUP
    blocks_per_chunk = OPROJ_ROWS // BLOCK

    def out_chunk(c):
        rows = slice(c * OPROJ_ROWS, (c + 1) * OPROJ_ROWS)
        att = jnp.concatenate([att_ref[c * blocks_per_chunk + j]
                               for j in range(blocks_per_chunk)], axis=1)
        mix = lax.dot_general(att, wo_ref[...], _TN, preferred_element_type=F32)
        o_ref[0, rows] = _layer_norm(DEEPNORM_ALPHA * x_ref[0, rows] + mix,
                                     ln_ref[0:1], ln_ref[1:2])

    for t in range(-1, nitems + 1):
        if 0 <= t - 1 < nitems:
            stage_c(t - 1)
        if 0 <= t < nitems:
            stage_b(t)
        if 0 <= t + 1 < nitems:
            stage_a(t + 1)
    for c in range(tm // OPROJ_ROWS):
        out_chunk(c)


def _attn_call(x, qt, k, vt, bias, sink, wo, ln, tm):
    B, S, D = x.shape
    nt = S // tm
    nblk = tm // BLOCK
    nsb = S // BLOCK
    main = lambda b, i: (b, i, 0)
    kprev = lambda b, i: (b, jnp.maximum(i * nblk - 1, 0), 0)
    knext = lambda b, i: (b, jnp.minimum((i + 1) * nblk, nsb - 1), 0)
    vprev = lambda b, i: (b, jnp.maximum(i * nblk - 1, 0), 0, 0)
    vnext = lambda b, i: (b, jnp.minimum((i + 1) * nblk, nsb - 1), 0, 0)
    return pl.pallas_call(
        functools.partial(_attn_kernel, tm=tm, nt=nt),
        grid=(B, nt),
        in_specs=[
            pl.BlockSpec((1, tm, D), main),
            pl.BlockSpec((1, nblk, D, BLOCK), lambda b, i: (b, i, 0, 0)),
            pl.BlockSpec((1, tm, KV_DIM), main),
            pl.BlockSpec((1, BLOCK, KV_DIM), kprev),
            pl.BlockSpec((1, BLOCK, KV_DIM), knext),
            pl.BlockSpec((1, nblk, KV_DIM, BLOCK), lambda b, i: (b, i, 0, 0)),
            pl.BlockSpec((1, 1, KV_DIM, BLOCK), vprev),
            pl.BlockSpec((1, 1, KV_DIM, BLOCK), vnext),
            _resident(bias.shape), _resident(sink.shape), _resident(wo.shape), _resident(ln.shape),
        ],
        out_specs=pl.BlockSpec((1, tm, D), main),
        out_shape=jax.ShapeDtypeStruct(x.shape, F32),
        scratch_shapes=[
            pltpu.VMEM((tm + 2 * BLOCK, KV_DIM), BF16),
            pltpu.VMEM((nblk + 2, KV_DIM, BLOCK), BF16),
            pltpu.VMEM((nblk, D, BLOCK), BF16),
            pltpu.VMEM((N_KV_HEADS, 3 * BLOCK, BLOCK), F32),
            pltpu.VMEM((N_KV_HEADS, 3 * BLOCK, BLOCK), BF16),
            pltpu.VMEM((1, N_KV_HEADS * BLOCK), F32),
        ],
        compiler_params=_params(),
        name="window_attn",
    )(x, qt, k, k, k, vt, vt, vt, bias, sink, wo, ln)


def _attn_bias():
    slopes = (2.0 ** (-8.0 * np.arange(1, N_HEADS + 1) / N_HEADS)).astype(np.float32)
    slopes = slopes.reshape(N_KV_HEADS, GROUP)
    kpos = np.arange(3 * BLOCK)[:, None] - BLOCK
    qpos = np.arange(BLOCK)[None, :]
    dist = np.abs(qpos - kpos)
    band = dist <= BLOCK
    out = np.empty((3, GROUP, N_KV_HEADS, 3 * BLOCK, BLOCK), np.float32)
    for variant in range(3):
        valid = band
        if variant == 0:
            valid = valid & (kpos >= 0)
        if variant == 2:
            valid = valid & (kpos < BLOCK)
        for r in range(GROUP):
            for g in range(N_KV_HEADS):
                out[variant, r, g] = np.where(valid, -slopes[g, r] * dist.astype(np.float32),
                                              np.float32(NEG_BIG))
    return jnp.asarray(out)


def _attn_weights(w_q, w_kv, w_o, sink):
    wqt = w_q.reshape(D_MODEL, N_KV_HEADS, GROUP, HEAD_DIM).transpose(2, 1, 3, 0)
    wqt = wqt.reshape(D_MODEL, D_MODEL).astype(BF16)
    wk = w_kv[:, :KV_DIM].astype(BF16)
    wvt = w_kv[:, KV_DIM:].T.astype(BF16)
    wo = w_o.reshape(N_KV_HEADS, GROUP, HEAD_DIM, D_MODEL).transpose(1, 0, 2, 3)
    wo = wo.reshape(D_MODEL, D_MODEL).astype(BF16)
    sk = sink.astype(F32).reshape(N_KV_HEADS, GROUP).T
    sk = jnp.repeat(sk, BLOCK, axis=1)[:, None, :]
    return wqt, wk, wvt, wo, sk


def _trunk(x, layers, tm):
    for kind, mixer, ffn in layers:
        if kind == "conv":
            x = _conv_call(x, *mixer, tm)
        else:
            wqt, wk, wvt, wo, sk, bias, ln = mixer
            qt, k, vt = _qkv_call(x, wqt, wk, wvt, tm)
            x = _attn_call(x, qt, k, vt, bias, sk, wo, ln, tm)
        x = _ffn_call(x, *ffn, tm)
    return x


def kernel(x_prompt, x_sample, conv_w_pw1, conv_b_pw1, conv_w_dw, conv_b_dw, conv_norm_g,
           conv_norm_b, conv_w_pw2, conv_b_pw2, attn_w_q, attn_w_kv, attn_w_o, attn_sink,
           ffn_w_up, ffn_w_dw, ffn_b_dw, ffn_w_down, ln_mix_g, ln_mix_b, ln_ffn_g, ln_ffn_b):
    bias = _attn_bias()
    layers = []
    for i in range(DEPTH):
        j = i // 2
        if i % 2 == 0:
            wa = conv_w_pw1[j][:, :D_MODEL].astype(BF16)
            wgt = conv_w_pw1[j][:, D_MODEL:].astype(BF16)
            dw = jnp.concatenate([conv_w_dw[j], conv_b_dw[j][None]], axis=0)
            dw = dw.reshape(CONV_WIDTH + 1, N_SLABS, LANES).transpose(1, 0, 2)
            vec = jnp.stack([conv_b_pw1[j][:D_MODEL], conv_b_pw1[j][D_MODEL:], conv_b_dw[j],
                             conv_norm_g[j], conv_norm_b[j], conv_b_pw2[j],
                             ln_mix_g[i], ln_mix_b[i]], axis=0)
            mixer = (wa, wgt, dw, conv_w_pw2[j].astype(BF16), vec)
            kind = "conv"
        else:
            wqt, wk, wvt, wo, sk = _attn_weights(attn_w_q[j], attn_w_kv[j], attn_w_o[j],
                                                 attn_sink[j])
            ln = jnp.stack([ln_mix_g[i], ln_mix_b[i]], axis=0)
            mixer = (wqt, wk, wvt, wo, sk, bias, ln)
            kind = "attn"
        ffn = _ffn_weights(ffn_w_up[i], ffn_w_dw[i], ffn_b_dw[i], ffn_w_down[i])
        ffn = ffn + (jnp.stack([ln_ffn_g[i], ln_ffn_b[i]], axis=0),)
        layers.append((kind, mixer, ffn))
    return (_trunk(x_prompt, layers, ROW_TILE), _trunk(x_sample, layers, ROW_TILE))
```

```python
import functools

import jax
import jax.numpy as jnp
import numpy as np
from jax import lax
from jax.experimental import pallas as pl
from jax.experimental.pallas import tpu as pltpu

D_MODEL = 1024
DEPTH = 4
CONV_WIDTH = 31
CONV_HALO = 16
N_HEADS = 16
HEAD_DIM = 64
N_KV_HEADS = 4
GROUP = N_HEADS // N_KV_HEADS
KV_DIM = N_KV_HEADS * HEAD_DIM
BLOCK = 128
D_FF = 2752
FFN_CHUNK = 256
D_FF_PAD = 2816
N_FFN_CHUNKS = D_FF_PAD // FFN_CHUNK
FFN_HALO = 8
FFN_OUT_ROWS = 512
FFN_PHASES = 4
LN_EPS = 1e-5
NEG_BIG = -1e30
DEEPNORM_ALPHA = (2.0 * DEPTH) ** 0.25

ROW_TILE = 1024
LANES = 128
N_SLABS = D_MODEL // LANES
CONV_ROWS = 64
CONV_PHASES = 4
CONV_OUT_ROWS = 512
OPROJ_ROWS = 256
VMEM_LIMIT = 56 * 1024 * 1024

F32 = jnp.float32
BF16 = jnp.bfloat16
_NT = (((1,), (1,)), ((), ()))
_TN = (((0,), (0,)), ((), ()))


def _layer_norm(y, g, b):
    mu = jnp.mean(y, axis=-1, keepdims=True)
    d = y - mu
    var = jnp.mean(d * d, axis=-1, keepdims=True)
    return d * lax.rsqrt(var + LN_EPS) * g + b


def _resident(shape):
    zeros = (0,) * len(shape)
    return pl.BlockSpec(shape, lambda b, i: zeros, pipeline_mode=pl.Buffered(1))


def _params():
    return pltpu.CompilerParams(
        dimension_semantics=("parallel", "arbitrary"),
        vmem_limit_bytes=VMEM_LIMIT)


def _ffn_kernel(xm_ref, xp_ref, xn_ref, wg_ref, wu_ref, cw_ref, wd_ref, ln_ref, o_ref,
                xe_ref, h_ref, gu_ref, act_ref, *, tm, nt):
    i = pl.program_id(1)
    xm = xm_ref[0]
    prev = jnp.where(i > 0, xp_ref[0], 0.0)
    nxt = jnp.where(i < nt - 1, xn_ref[0], 0.0)
    xe_ref[...] = jnp.concatenate([prev, xm, nxt], axis=0).astype(BF16)
    qrows = tm // FFN_PHASES
    slabs = FFN_CHUNK // LANES

    def conv3(par, slab, phase, cw, base):
        taps = [gu_ref[par, slab, pl.ds(FFN_HALO + d + phase, qrows, stride=FFN_PHASES), :]
                for d in (-1, 0, 1)]
        return (cw[base:base + 1] * taps[0] + cw[base + 1:base + 2] * taps[1]
                + cw[base + 2:base + 3] * taps[2] + cw[base + 3:base + 4])

    for j in range(N_FFN_CHUNKS):
        xe = xe_ref[...]
        par = j % 2
        sl = slice(j * FFN_CHUNK, (j + 1) * FFN_CHUNK)
        g_ext = jnp.dot(xe, wg_ref[:, sl], preferred_element_type=F32)
        u_ext = jnp.dot(xe, wu_ref[:, sl], preferred_element_type=F32)
        for hh in range(slabs):
            gu_ref[par, hh] = g_ext[:, hh * LANES:(hh + 1) * LANES]
            gu_ref[par, slabs + hh] = u_ext[:, hh * LANES:(hh + 1) * LANES]
        for hh in range(slabs):
            cw = cw_ref[:, j * FFN_CHUNK + hh * LANES:j * FFN_CHUNK + (hh + 1) * LANES]
            for phase in range(FFN_PHASES):
                g = conv3(par, hh, phase, cw, 0)
                u = conv3(par, slabs + hh, phase, cw, 4)
                act_ref[par, hh, pl.ds(phase, qrows, stride=FFN_PHASES), :] = (
                    (g * jax.nn.sigmoid(g)) * u)
        h_ref[:, sl] = jnp.concatenate([act_ref[par, hh] for hh in range(slabs)],
                                       axis=1).astype(BF16)

    for c in range(tm // FFN_OUT_ROWS):
        rows_c = slice(c * FFN_OUT_ROWS, (c + 1) * FFN_OUT_ROWS)
        f = jnp.dot(h_ref[rows_c, :], wd_ref[...], preferred_element_type=F32)
        o_ref[0, rows_c] = _layer_norm(DEEPNORM_ALPHA * xm_ref[0, rows_c] + f,
                                       ln_ref[0:1], ln_ref[1:2])


def _ffn_call(x, wg, wu, cw, wd, ln, tm):
    B, S, D = x.shape
    nt = S // tm
    hb = tm // FFN_HALO
    nhb = S // FFN_HALO
    return pl.pallas_call(
        functools.partial(_ffn_kernel, tm=tm, nt=nt),
        grid=(B, nt),
        in_specs=[
            pl.BlockSpec((1, tm, D), lambda b, i: (b, i, 0)),
            pl.BlockSpec((1, FFN_HALO, D), lambda b, i: (b, jnp.maximum(i * hb - 1, 0), 0)),
            pl.BlockSpec((1, FFN_HALO, D), lambda b, i: (b, jnp.minimum((i + 1) * hb, nhb - 1), 0)),
            _resident(wg.shape), _resident(wu.shape), _resident(cw.shape),
            _resident(wd.shape), _resident(ln.shape),
        ],
        out_specs=pl.BlockSpec((1, tm, D), lambda b, i: (b, i, 0)),
        out_shape=jax.ShapeDtypeStruct(x.shape, F32),
        scratch_shapes=[
            pltpu.VMEM((tm + 2 * FFN_HALO, D), BF16),
            pltpu.VMEM((tm, D_FF_PAD), BF16),
            pltpu.VMEM((2, 2 * FFN_CHUNK // LANES, tm + 2 * FFN_HALO, LANES), F32),
            pltpu.VMEM((2, FFN_CHUNK // LANES, tm, LANES), F32),
        ],
        compiler_params=_params(),
        name="conv_ffn",
    )(x, x, x, wg, wu, cw, wd, ln)


def _ffn_weights(w_up, w_dw, b_dw, w_down):
    pad = D_FF_PAD - D_FF
    wg = jnp.pad(w_up[:, :D_FF], ((0, 0), (0, pad))).astype(BF16)
    wu = jnp.pad(w_up[:, D_FF:], ((0, 0), (0, pad))).astype(BF16)
    taps = jnp.concatenate([w_dw[:, :D_FF], b_dw[None, :D_FF],
                            w_dw[:, D_FF:], b_dw[None, D_FF:]], axis=0)
    cw = jnp.pad(taps, ((0, 0), (0, pad)))
    wd = jnp.pad(w_down, ((0, pad), (0, 0))).astype(BF16)
    return wg, wu, cw, wd


def _conv_kernel(xm_ref, xp_ref, xn_ref, wa_ref, wgt_ref, dw_ref, w2_ref, vec_ref, o_ref,
                 hg_ref, c_ref, *, tm, nt):
    i = pl.program_id(1)
    xm = xm_ref[0]
    rows = tm + 2 * CONV_HALO
    xe = jnp.concatenate([xp_ref[0], xm, xn_ref[0]], axis=0).astype(BF16)
    for c in range(D_MODEL // 256):
        sl = slice(c * 256, (c + 1) * 256)
        a = jnp.dot(xe, wa_ref[:, sl], preferred_element_type=F32) + vec_ref[0:1, sl]
        gt = jnp.dot(xe, wgt_ref[:, sl], preferred_element_type=F32) + vec_ref[1:2, sl]
        glu = a * jax.nn.sigmoid(gt)
        hg_ref[2 * c] = glu[:, :LANES]
        hg_ref[2 * c + 1] = glu[:, LANES:]

    @pl.when(i == 0)
    def _():
        hg_ref[:, 0:CONV_HALO, :] = jnp.zeros((N_SLABS, CONV_HALO, LANES), F32)

    @pl.when(i == nt - 1)
    def _():
        hg_ref[:, CONV_HALO + tm:rows, :] = jnp.zeros((N_SLABS, CONV_HALO, LANES), F32)

    q = CONV_ROWS // CONV_PHASES

    def row_block(rb, carry):
        r0 = rb * CONV_ROWS
        for c in range(N_SLABS):
            for j in range(CONV_PHASES):
                acc = jnp.broadcast_to(dw_ref[c, CONV_WIDTH:CONV_WIDTH + 1, :], (q, LANES))
                for k in range(CONV_WIDTH):
                    tap = hg_ref[c, pl.ds(r0 + j + k + 1, q, stride=CONV_PHASES), :]
                    acc = acc + dw_ref[c, k:k + 1, :] * tap
                c_ref[c, pl.ds(r0 + j, q, stride=CONV_PHASES), :] = acc
        return carry

    lax.fori_loop(0, tm // CONV_ROWS, row_block, 0)

    for r0 in range(0, tm, CONV_OUT_ROWS):
        rs = slice(r0, r0 + CONV_OUT_ROWS)
        conv = jnp.concatenate([c_ref[c, rs, :] for c in range(N_SLABS)], axis=1)
        y = _layer_norm(conv, vec_ref[3:4], vec_ref[4:5])
        s = (y * jax.nn.sigmoid(y)).astype(BF16)
        mix = jnp.dot(s, w2_ref[...], preferred_element_type=F32) + vec_ref[5:6]
        o_ref[0, rs] = _layer_norm(DEEPNORM_ALPHA * xm_ref[0, rs] + mix,
                                   vec_ref[6:7], vec_ref[7:8])


def _conv_call(x, wa, wgt, dw, w2, vec, tm):
    B, S, D = x.shape
    nt = S // tm
    hb = tm // CONV_HALO
    nhb = S // CONV_HALO
    return pl.pallas_call(
        functools.partial(_conv_kernel, tm=tm, nt=nt),
        grid=(B, nt),
        in_specs=[
            pl.BlockSpec((1, tm, D), lambda b, i: (b, i, 0)),
            pl.BlockSpec((1, CONV_HALO, D), lambda b, i: (b, jnp.maximum(i * hb - 1, 0), 0)),
            pl.BlockSpec((1, CONV_HALO, D), lambda b, i: (b, jnp.minimum((i + 1) * hb, nhb - 1), 0)),
            _resident(wa.shape), _resident(wgt.shape), _resident(dw.shape),
            _resident(w2.shape), _resident(vec.shape),
        ],
        out_specs=pl.BlockSpec((1, tm, D), lambda b, i: (b, i, 0)),
        out_shape=jax.ShapeDtypeStruct(x.shape, F32),
        scratch_shapes=[
            pltpu.VMEM((N_SLABS, tm + 2 * CONV_HALO, LANES), F32),
            pltpu.VMEM((N_SLABS, tm, LANES), F32),
        ],
        compiler_params=_params(),
        name="conformer_conv",
    )(x, x, x, wa, wgt, dw, w2, vec)


def _qkv_kernel(x_ref, wqt_ref, wk_ref, wvt_ref, qt_ref, k_ref, vt_ref, *, tm):
    x = x_ref[0].astype(BF16)
    qt = lax.dot_general(wqt_ref[...], x, _NT, preferred_element_type=F32) * (HEAD_DIM ** -0.5)
    qt = qt.astype(BF16)
    vt = lax.dot_general(wvt_ref[...], x, _NT, preferred_element_type=F32).astype(BF16)
    for jb in range(tm // BLOCK):
        qt_ref[0, jb] = qt[:, jb * BLOCK:(jb + 1) * BLOCK]
        vt_ref[0, jb] = vt[:, jb * BLOCK:(jb + 1) * BLOCK]
    k_ref[0] = jnp.dot(x, wk_ref[...], preferred_element_type=F32).astype(BF16)


def _qkv_call(x, wqt, wk, wvt, tm):
    B, S, D = x.shape
    nblk = tm // BLOCK
    return pl.pallas_call(
        functools.partial(_qkv_kernel, tm=tm),
        grid=(B, S // tm),
        in_specs=[
            pl.BlockSpec((1, tm, D), lambda b, i: (b, i, 0)),
            _resident(wqt.shape), _resident(wk.shape), _resident(wvt.shape),
        ],
        out_specs=[
            pl.BlockSpec((1, nblk, D, BLOCK), lambda b, i: (b, i, 0, 0)),
            pl.BlockSpec((1, tm, KV_DIM), lambda b, i: (b, i, 0)),
            pl.BlockSpec((1, nblk, KV_DIM, BLOCK), lambda b, i: (b, i, 0, 0)),
        ],
        out_shape=[
            jax.ShapeDtypeStruct((B, S // BLOCK, D, BLOCK), BF16),
            jax.ShapeDtypeStruct((B, S, KV_DIM), BF16),
            jax.ShapeDtypeStruct((B, S // BLOCK, KV_DIM, BLOCK), BF16),
        ],
        compiler_params=_params(),
        name="qkv_proj",
    )(x, wqt, wk, wvt)


def _attn_kernel(x_ref, qt_ref, km_ref, kp_ref, kn_ref, vm_ref, vp_ref, vn_ref,
                 bias_ref, sink_ref, wo_ref, ln_ref, o_ref,
                 kcat_ref, vblk_ref, att_ref, s_ref, p_ref, inv_ref, *, tm, nt):
    i = pl.program_id(1)
    nblk = tm // BLOCK
    last_blk = nt * nblk - 1
    kcat_ref[0:BLOCK] = kp_ref[0]
    kcat_ref[BLOCK:BLOCK + tm] = km_ref[0]
    kcat_ref[BLOCK + tm:2 * BLOCK + tm] = kn_ref[0]
    vblk_ref[0] = vp_ref[0, 0]
    for jb in range(nblk):
        vblk_ref[jb + 1] = vm_ref[0, jb]
    vblk_ref[nblk + 1] = vn_ref[0, 0]

    def stage_a(t):
        jb, r = divmod(t, GROUP)
        row0 = jb * BLOCK
        qt = qt_ref[0, jb, r * KV_DIM:(r + 1) * KV_DIM, :]
        zero = jnp.zeros((HEAD_DIM, BLOCK), BF16)
        qs = jnp.concatenate(
            [jnp.concatenate([qt[g * HEAD_DIM:(g + 1) * HEAD_DIM] if h == g else zero
                              for h in range(N_KV_HEADS)], axis=0)
             for g in range(N_KV_HEADS)], axis=1)
        kband = kcat_ref[row0:row0 + 3 * BLOCK, :]
        s = jnp.dot(kband, qs, preferred_element_type=F32)
        for g in range(N_KV_HEADS):
            s_ref[g] = s[:, g * BLOCK:(g + 1) * BLOCK]

    def stage_b(t):
        jb, r = divmod(t, GROUP)
        blk = i * nblk + jb
        variant = jnp.where(blk == 0, 0, jnp.where(blk == last_blk, 2, 1))
        for g in range(N_KV_HEADS):
            cs = slice(g * BLOCK, (g + 1) * BLOCK)
            s = s_ref[g] + bias_ref[variant, r, g]
            sink = sink_ref[r, :, cs]
            m = jnp.maximum(jnp.max(s, axis=0, keepdims=True), sink)
            p = jnp.exp(s - m)
            inv_ref[:, cs] = 1.0 / (jnp.sum(p, axis=0, keepdims=True) + jnp.exp(sink - m))
            p_ref[g] = p.astype(BF16)

    def stage_c(t):
        jb, r = divmod(t, GROUP)
        vband = jnp.concatenate([vblk_ref[jb], vblk_ref[jb + 1], vblk_ref[jb + 2]], axis=1)
        inv = inv_ref[...]
        out = jnp.concatenate(
            [jnp.dot(vband[g * HEAD_DIM:(g + 1) * HEAD_DIM], p_ref[g],
                     preferred_element_type=F32) * inv[:, g * BLOCK:(g + 1) * BLOCK]
             for g in range(N_KV_HEADS)], axis=0)
        att_ref[jb, r * KV_DIM:(r + 1) * KV_DIM, :] = out.astype(BF16)

    nitems = nblk * GROUP
    blocks_per_chunk = OPROJ_ROWS // BLOCK

    def out_chunk(c):
        rows = slice(c * OPROJ_ROWS, (c + 1) * OPROJ_ROWS)
        att = jnp.concatenate([att_ref[c * blocks_per_chunk + j]
                               for j in range(blocks_per_chunk)], axis=1)
        mix = lax.dot_general(att, wo_ref[...], _TN, preferred_element_type=F32)
        o_ref[0, rows] = _layer_norm(DEEPNORM_ALPHA * x_ref[0, rows] + mix,
                                     ln_ref[0:1], ln_ref[1:2])

    for t in range(-1, nitems + 1):
        if 0 <= t - 1 < nitems:
            stage_c(t - 1)
        if 0 <= t < nitems:
            stage_b(t)
        if 0 <= t + 1 < nitems:
            stage_a(t + 1)
    for c in range(tm // OPROJ_ROWS):
        out_chunk(c)


def _attn_call(x, qt, k, vt, bias, sink, wo, ln, tm):
    B, S, D = x.shape
    nt = S // tm
    nblk = tm // BLOCK
    nsb = S // BLOCK
    main = lambda b, i: (b, i, 0)
    kprev = lambda b, i: (b, jnp.maximum(i * nblk - 1, 0), 0)
    knext = lambda b, i: (b, jnp.minimum((i + 1) * nblk, nsb - 1), 0)
    vprev = lambda b, i: (b, jnp.maximum(i * nblk - 1, 0), 0, 0)
    vnext = lambda b, i: (b, jnp.minimum((i + 1) * nblk, nsb - 1), 0, 0)
    return pl.pallas_call(
        functools.partial(_attn_kernel, tm=tm, nt=nt),
        grid=(B, nt),
        in_specs=[
            pl.BlockSpec((1, tm, D), main),
            pl.BlockSpec((1, nblk, D, BLOCK), lambda b, i: (b, i, 0, 0)),
            pl.BlockSpec((1, tm, KV_DIM), main),
            pl.BlockSpec((1, BLOCK, KV_DIM), kprev),
            pl.BlockSpec((1, BLOCK, KV_DIM), knext),
            pl.BlockSpec((1, nblk, KV_DIM, BLOCK), lambda b, i: (b, i, 0, 0)),
            pl.BlockSpec((1, 1, KV_DIM, BLOCK), vprev),
            pl.BlockSpec((1, 1, KV_DIM, BLOCK), vnext),
            _resident(bias.shape), _resident(sink.shape), _resident(wo.shape), _resident(ln.shape),
        ],
        out_specs=pl.BlockSpec((1, tm, D), main),
        out_shape=jax.ShapeDtypeStruct(x.shape, F32),
        scratch_shapes=[
            pltpu.VMEM((tm + 2 * BLOCK, KV_DIM), BF16),
            pltpu.VMEM((nblk + 2, KV_DIM, BLOCK), BF16),
            pltpu.VMEM((nblk, D, BLOCK), BF16),
            pltpu.VMEM((N_KV_HEADS, 3 * BLOCK, BLOCK), F32),
            pltpu.VMEM((N_KV_HEADS, 3 * BLOCK, BLOCK), BF16),
            pltpu.VMEM((1, N_KV_HEADS * BLOCK), F32),
        ],
        compiler_params=_params(),
        name="window_attn",
    )(x, qt, k, k, k, vt, vt, vt, bias, sink, wo, ln)


def _attn_bias():
    slopes = (2.0 ** (-8.0 * np.arange(1, N_HEADS + 1) / N_HEADS)).astype(np.float32)
    slopes = slopes.reshape(N_KV_HEADS, GROUP)
    kpos = np.arange(3 * BLOCK)[:, None] - BLOCK
    qpos = np.arange(BLOCK)[None, :]
    dist = np.abs(qpos - kpos)
    band = dist <= BLOCK
    out = np.empty((3, GROUP, N_KV_HEADS, 3 * BLOCK, BLOCK), np.float32)
    for variant in range(3):
        valid = band
        if variant == 0:
            valid = valid & (kpos >= 0)
        if variant == 2:
            valid = valid & (kpos < BLOCK)
        for r in range(GROUP):
            for g in range(N_KV_HEADS):
                out[variant, r, g] = np.where(valid, -slopes[g, r] * dist.astype(np.float32),
                                              np.float32(NEG_BIG))
    return jnp.asarray(out)


def _attn_weights(w_q, w_kv, w_o, sink):
    wqt = w_q.reshape(D_MODEL, N_KV_HEADS, GROUP, HEAD_DIM).transpose(2, 1, 3, 0)
    wqt = wqt.reshape(D_MODEL, D_MODEL).astype(BF16)
    wk = w_kv[:, :KV_DIM].astype(BF16)
    wvt = w_kv[:, KV_DIM:].T.astype(BF16)
    wo = w_o.reshape(N_KV_HEADS, GROUP, HEAD_DIM, D_MODEL).transpose(1, 0, 2, 3)
    wo = wo.reshape(D_MODEL, D_MODEL).astype(BF16)
    sk = sink.astype(F32).reshape(N_KV_HEADS, GROUP).T
    sk = jnp.repeat(sk, BLOCK, axis=1)[:, None, :]
    return wqt, wk, wvt, wo, sk


def _trunk(x, layers, tm):
    for kind, mixer, ffn in layers:
        if kind == "conv":
            x = _conv_call(x, *mixer, tm)
        else:
            wqt, wk, wvt, wo, sk, bias, ln = mixer
            qt, k, vt = _qkv_call(x, wqt, wk, wvt, tm)
            x = _attn_call(x, qt, k, vt, bias, sk, wo, ln, tm)
        x = _ffn_call(x, *ffn, tm)
    return x


def kernel(x_prompt, x_sample, conv_w_pw1, conv_b_pw1, conv_w_dw, conv_b_dw, conv_norm_g,
           conv_norm_b, conv_w_pw2, conv_b_pw2, attn_w_q, attn_w_kv, attn_w_o, attn_sink,
           ffn_w_up, ffn_w_dw, ffn_b_dw, ffn_w_down, ln_mix_g, ln_mix_b, ln_ffn_g, ln_ffn_b):
    bias = _attn_bias()
    layers = []
    for i in range(DEPTH):
        j = i // 2
        if i % 2 == 0:
            wa = conv_w_pw1[j][:, :D_MODEL].astype(BF16)
            wgt = conv_w_pw1[j][:, D_MODEL:].astype(BF16)
            dw = jnp.concatenate([conv_w_dw[j], conv_b_dw[j][None]], axis=0)
            dw = dw.reshape(CONV_WIDTH + 1, N_SLABS, LANES).transpose(1, 0, 2)
            vec = jnp.stack([conv_b_pw1[j][:D_MODEL], conv_b_pw1[j][D_MODEL:], conv_b_dw[j],
                             conv_norm_g[j], conv_norm_b[j], conv_b_pw2[j],
                             ln_mix_g[i], ln_mix_b[i]], axis=0)
            mixer = (wa, wgt, dw, conv_w_pw2[j].astype(BF16), vec)
            kind = "conv"
        else:
            wqt, wk, wvt, wo, sk = _attn_weights(attn_w_q[j], attn_w_kv[j], attn_w_o[j],
                                                 attn_sink[j])
            ln = jnp.stack([ln_mix_g[i], ln_mix_b[i]], axis=0)
            mixer = (wqt, wk, wvt, wo, sk, bias, ln)
            kind = "attn"
        ffn = _ffn_weights(ffn_w_up[i], ffn_w_dw[i], ffn_b_dw[i], ffn_w_down[i])
        ffn = ffn + (jnp.stack([ln_ffn_g[i], ln_ffn_b[i]], axis=0),)
        layers.append((kind, mixer, ffn))
    return (_trunk(x_prompt, layers, ROW_TILE), _trunk(x_sample, layers, ROW_TILE))
```

```python
import functools

import jax
import jax.numpy as jnp
import numpy as np
from jax import lax
from jax.experimental import pallas as pl
from jax.experimental.pallas import tpu as pltpu

D_MODEL = 1024
DEPTH = 4
CONV_WIDTH = 31
CONV_HALO = 16
N_HEADS = 16
HEAD_DIM = 64
N_KV_HEADS = 4
GROUP = N_HEADS // N_KV_HEADS
KV_DIM = N_KV_HEADS * HEAD_DIM
BLOCK = 128
D_FF = 2752
FFN_CHUNK = 256
D_FF_PAD = 2816
N_FFN_CHUNKS = D_FF_PAD // FFN_CHUNK
FFN_HALO = 8
FFN_OUT_ROWS = 256
FFN_PHASES = 4
LN_EPS = 1e-5
NEG_BIG = -1e30
DEEPNORM_ALPHA = (2.0 * DEPTH) ** 0.25

ROW_TILE = 1024
LANES = 128
N_SLABS = D_MODEL // LANES
CONV_ROWS = 128
CONV_PHASES = 4
CONV_OUT_ROWS = 512
OPROJ_ROWS = 256
VMEM_LIMIT = 56 * 1024 * 1024

F32 = jnp.float32
BF16 = jnp.bfloat16
_NT = (((1,), (1,)), ((), ()))
_TN = (((0,), (0,)), ((), ()))


def _layer_norm(y, g, b):
    mu = jnp.mean(y, axis=-1, keepdims=True)
    d = y - mu
    var = jnp.mean(d * d, axis=-1, keepdims=True)
    return d * lax.rsqrt(var + LN_EPS) * g + b


def _resident(shape):
    zeros = (0,) * len(shape)
    return pl.BlockSpec(shape, lambda b, i: zeros, pipeline_mode=pl.Buffered(1))


def _params():
    return pltpu.CompilerParams(
        dimension_semantics=("parallel", "arbitrary"),
        vmem_limit_bytes=VMEM_LIMIT)


def _ffn_kernel(xm_ref, xp_ref, xn_ref, wg_ref, wu_ref, cw_ref, wd_ref, ln_ref, o_ref,
                xe_ref, h_ref, gu_ref, act_ref, *, tm, nt):
    i = pl.program_id(1)
    xm = xm_ref[0]
    prev = jnp.where(i > 0, xp_ref[0], 0.0)
    nxt = jnp.where(i < nt - 1, xn_ref[0], 0.0)
    xe_ref[...] = jnp.concatenate([prev, xm, nxt], axis=0).astype(BF16)
    qrows = tm // FFN_PHASES
    slabs = FFN_CHUNK // LANES

    def conv3(par, slab, phase, cw, base):
        taps = [gu_ref[par, slab, pl.ds(FFN_HALO + d + phase, qrows, stride=FFN_PHASES), :]
                for d in (-1, 0, 1)]
        return (cw[base:base + 1] * taps[0] + cw[base + 1:base + 2] * taps[1]
                + cw[base + 2:base + 3] * taps[2] + cw[base + 3:base + 4])

    for j in range(N_FFN_CHUNKS):
        xe = xe_ref[...]
        par = j % 2
        sl = slice(j * FFN_CHUNK, (j + 1) * FFN_CHUNK)
        g_ext = jnp.dot(xe, wg_ref[:, sl], preferred_element_type=F32)
        u_ext = jnp.dot(xe, wu_ref[:, sl], preferred_element_type=F32)
        for hh in range(slabs):
            gu_ref[par, hh] = g_ext[:, hh * LANES:(hh + 1) * LANES]
            gu_ref[par, slabs + hh] = u_ext[:, hh * LANES:(hh + 1) * LANES]
        for hh in range(slabs):
            cw = cw_ref[:, j * FFN_CHUNK + hh * LANES:j * FFN_CHUNK + (hh + 1) * LANES]
            for phase in range(FFN_PHASES):
                g = conv3(par, hh, phase, cw, 0)
                u = conv3(par, slabs + hh, phase, cw, 4)
                act_ref[par, hh, pl.ds(phase, qrows, stride=FFN_PHASES), :] = (
                    (g * jax.nn.sigmoid(g)) * u)
        h_ref[:, sl] = jnp.concatenate([act_ref[par, hh] for hh in range(slabs)],
                                       axis=1).astype(BF16)

    for c in range(tm // FFN_OUT_ROWS):
        rows_c = slice(c * FFN_OUT_ROWS, (c + 1) * FFN_OUT_ROWS)
        f = jnp.dot(h_ref[rows_c, :], wd_ref[...], preferred_element_type=F32)
        o_ref[0, rows_c] = _layer_norm(DEEPNORM_ALPHA * xm_ref[0, rows_c] + f,
                                       ln_ref[0:1], ln_ref[1:2])


def _ffn_call(x, wg, wu, cw, wd, ln, tm):
    B, S, D = x.shape
    nt = S // tm
    hb = tm // FFN_HALO
    nhb = S // FFN_HALO
    return pl.pallas_call(
        functools.partial(_ffn_kernel, tm=tm, nt=nt),
        grid=(B, nt),
        in_specs=[
            pl.BlockSpec((1, tm, D), lambda b, i: (b, i, 0)),
            pl.BlockSpec((1, FFN_HALO, D), lambda b, i: (b, jnp.maximum(i * hb - 1, 0), 0)),
            pl.BlockSpec((1, FFN_HALO, D), lambda b, i: (b, jnp.minimum((i + 1) * hb, nhb - 1), 0)),
            _resident(wg.shape), _resident(wu.shape), _resident(cw.shape),
            _resident(wd.shape), _resident(ln.shape),
        ],
        out_specs=pl.BlockSpec((1, tm, D), lambda b, i: (b, i, 0)),
        out_shape=jax.ShapeDtypeStruct(x.shape, F32),
        scratch_shapes=[
            pltpu.VMEM((tm + 2 * FFN_HALO, D), BF16),
            pltpu.VMEM((tm, D_FF_PAD), BF16),
            pltpu.VMEM((2, 2 * FFN_CHUNK // LANES, tm + 2 * FFN_HALO, LANES), F32),
            pltpu.VMEM((2, FFN_CHUNK // LANES, tm, LANES), F32),
        ],
        compiler_params=_params(),
        name="conv_ffn",
    )(x, x, x, wg, wu, cw, wd, ln)


def _ffn_weights(w_up, w_dw, b_dw, w_down):
    pad = D_FF_PAD - D_FF
    wg = jnp.pad(w_up[:, :D_FF], ((0, 0), (0, pad))).astype(BF16)
    wu = jnp.pad(w_up[:, D_FF:], ((0, 0), (0, pad))).astype(BF16)
    taps = jnp.concatenate([w_dw[:, :D_FF], b_dw[None, :D_FF],
                            w_dw[:, D_FF:], b_dw[None, D_FF:]], axis=0)
    cw = jnp.pad(taps, ((0, 0), (0, pad)))
    wd = jnp.pad(w_down, ((0, pad), (0, 0))).astype(BF16)
    return wg, wu, cw, wd


def _conv_kernel(xm_ref, xp_ref, xn_ref, wa_ref, wgt_ref, dw_ref, w2_ref, vec_ref, o_ref,
                 hg_ref, c_ref, *, tm, nt):
    i = pl.program_id(1)
    xm = xm_ref[0]
    rows = tm + 2 * CONV_HALO
    xe = jnp.concatenate([xp_ref[0], xm, xn_ref[0]], axis=0).astype(BF16)
    for c in range(D_MODEL // 256):
        sl = slice(c * 256, (c + 1) * 256)
        a = jnp.dot(xe, wa_ref[:, sl], preferred_element_type=F32) + vec_ref[0:1, sl]
        gt = jnp.dot(xe, wgt_ref[:, sl], preferred_element_type=F32) + vec_ref[1:2, sl]
        glu = a * jax.nn.sigmoid(gt)
        hg_ref[2 * c] = glu[:, :LANES]
        hg_ref[2 * c + 1] = glu[:, LANES:]

    @pl.when(i == 0)
    def _():
        hg_ref[:, 0:CONV_HALO, :] = jnp.zeros((N_SLABS, CONV_HALO, LANES), F32)

    @pl.when(i == nt - 1)
    def _():
        hg_ref[:, CONV_HALO + tm:rows, :] = jnp.zeros((N_SLABS, CONV_HALO, LANES), F32)

    q = CONV_ROWS // CONV_PHASES

    def row_block(rb, carry):
        r0 = rb * CONV_ROWS
        for c in range(N_SLABS):
            for j in range(CONV_PHASES):
                acc = jnp.broadcast_to(dw_ref[c, CONV_WIDTH:CONV_WIDTH + 1, :], (q, LANES))
                for k in range(CONV_WIDTH):
                    tap = hg_ref[c, pl.ds(r0 + j + k + 1, q, stride=CONV_PHASES), :]
                    acc = acc + dw_ref[c, k:k + 1, :] * tap
                c_ref[c, pl.ds(r0 + j, q, stride=CONV_PHASES), :] = acc
        return carry

    lax.fori_loop(0, tm // CONV_ROWS, row_block, 0)

    for r0 in range(0, tm, CONV_OUT_ROWS):
        rs = slice(r0, r0 + CONV_OUT_ROWS)
        conv = jnp.concatenate([c_ref[c, rs, :] for c in range(N_SLABS)], axis=1)
        y = _layer_norm(conv, vec_ref[3:4], vec_ref[4:5])
        s = (y * jax.nn.sigmoid(y)).astype(BF16)
        mix = jnp.dot(s, w2_ref[...], preferred_element_type=F32) + vec_ref[5:6]
        o_ref[0, rs] = _layer_norm(DEEPNORM_ALPHA * xm_ref[0, rs] + mix,
                                   vec_ref[6:7], vec_ref[7:8])


def _conv_call(x, wa, wgt, dw, w2, vec, tm):
    B, S, D = x.shape
    nt = S // tm
    hb = tm // CONV_HALO
    nhb = S // CONV_HALO
    return pl.pallas_call(
        functools.partial(_conv_kernel, tm=tm, nt=nt),
        grid=(B, nt),
        in_specs=[
            pl.BlockSpec((1, tm, D), lambda b, i: (b, i, 0)),
            pl.BlockSpec((1, CONV_HALO, D), lambda b, i: (b, jnp.maximum(i * hb - 1, 0), 0)),
            pl.BlockSpec((1, CONV_HALO, D), lambda b, i: (b, jnp.minimum((i + 1) * hb, nhb - 1), 0)),
            _resident(wa.shape), _resident(wgt.shape), _resident(dw.shape),
            _resident(w2.shape), _resident(vec.shape),
        ],
        out_specs=pl.BlockSpec((1, tm, D), lambda b, i: (b, i, 0)),
        out_shape=jax.ShapeDtypeStruct(x.shape, F32),
        scratch_shapes=[
            pltpu.VMEM((N_SLABS, tm + 2 * CONV_HALO, LANES), F32),
            pltpu.VMEM((N_SLABS, tm, LANES), F32),
        ],
        compiler_params=_params(),
        name="conformer_conv",
    )(x, x, x, wa, wgt, dw, w2, vec)


def _qkv_kernel(x_ref, wqt_ref, wk_ref, wvt_ref, qt_ref, k_ref, vt_ref, *, tm):
    x = x_ref[0].astype(BF16)
    qt = lax.dot_general(wqt_ref[...], x, _NT, preferred_element_type=F32) * (HEAD_DIM ** -0.5)
    qt = qt.astype(BF16)
    vt = lax.dot_general(wvt_ref[...], x, _NT, preferred_element_type=F32).astype(BF16)
    for jb in range(tm // BLOCK):
        qt_ref[0, jb] = qt[:, jb * BLOCK:(jb + 1) * BLOCK]
        vt_ref[0, jb] = vt[:, jb * BLOCK:(jb + 1) * BLOCK]
    k_ref[0] = jnp.dot(x, wk_ref[...], preferred_element_type=F32).astype(BF16)


def _qkv_call(x, wqt, wk, wvt, tm):
    B, S, D = x.shape
    nblk = tm // BLOCK
    return pl.pallas_call(
        functools.partial(_qkv_kernel, tm=tm),
        grid=(B, S // tm),
        in_specs=[
            pl.BlockSpec((1, tm, D), lambda b, i: (b, i, 0)),
            _resident(wqt.shape), _resident(wk.shape), _resident(wvt.shape),
        ],
        out_specs=[
            pl.BlockSpec((1, nblk, D, BLOCK), lambda b, i: (b, i, 0, 0)),
            pl.BlockSpec((1, tm, KV_DIM), lambda b, i: (b, i, 0)),
            pl.BlockSpec((1, nblk, KV_DIM, BLOCK), lambda b, i: (b, i, 0, 0)),
        ],
        out_shape=[
            jax.ShapeDtypeStruct((B, S // BLOCK, D, BLOCK), BF16),
            jax.ShapeDtypeStruct((B, S, KV_DIM), BF16),
            jax.ShapeDtypeStruct((B, S // BLOCK, KV_DIM, BLOCK), BF16),
        ],
        compiler_params=_params(),
        name="qkv_proj",
    )(x, wqt, wk, wvt)


def _attn_kernel(x_ref, qt_ref, km_ref, kp_ref, kn_ref, vm_ref, vp_ref, vn_ref,
                 bias_ref, sink_ref, wo_ref, ln_ref, o_ref,
                 kcat_ref, vblk_ref, att_ref, s_ref, p_ref, inv_ref, *, tm, nt):
    i = pl.program_id(1)
    nblk = tm // BLOCK
    last_blk = nt * nblk - 1
    kcat_ref[0:BLOCK] = kp_ref[0]
    kcat_ref[BLOCK:BLOCK + tm] = km_ref[0]
    kcat_ref[BLOCK + tm:2 * BLOCK + tm] = kn_ref[0]
    vblk_ref[0] = vp_ref[0, 0]
    for jb in range(nblk):
        vblk_ref[jb + 1] = vm_ref[0, jb]
    vblk_ref[nblk + 1] = vn_ref[0, 0]

    def stage_a(t):
        jb, r = divmod(t, GROUP)
        row0 = jb * BLOCK
        qt = qt_ref[0, jb, r * KV_DIM:(r + 1) * KV_DIM, :]
        zero = jnp.zeros((HEAD_DIM, BLOCK), BF16)
        qs = jnp.concatenate(
            [jnp.concatenate([qt[g * HEAD_DIM:(g + 1) * HEAD_DIM] if h == g else zero
                              for h in range(N_KV_HEADS)], axis=0)
             for g in range(N_KV_HEADS)], axis=1)
        kband = kcat_ref[row0:row0 + 3 * BLOCK, :]
        s = jnp.dot(kband, qs, preferred_element_type=F32)
        for g in range(N_KV_HEADS):
            s_ref[g] = s[:, g * BLOCK:(g + 1) * BLOCK]

    def stage_b(t):
        jb, r = divmod(t, GROUP)
        blk = i * nblk + jb
        variant = jnp.where(blk == 0, 0, jnp.where(blk == last_blk, 2, 1))
        for g in range(N_KV_HEADS):
            cs = slice(g * BLOCK, (g + 1) * BLOCK)
            s = s_ref[g] + bias_ref[variant, r, g]
            sink = sink_ref[r, :, cs]
            m = jnp.maximum(jnp.max(s, axis=0, keepdims=True), sink)
            p = jnp.exp(s - m)
            inv_ref[:, cs] = 1.0 / (jnp.sum(p, axis=0, keepdims=True) + jnp.exp(sink - m))
            p_ref[g] = p.astype(BF16)

    def stage_c(t):
        jb, r = divmod(t, GROUP)
        vband = jnp.concatenate([vblk_ref[jb], vblk_ref[jb + 1], vblk_ref[jb + 2]], axis=1)
        inv = inv_ref[...]
        out = jnp.concatenate(
            [jnp.dot(vband[g * HEAD_DIM:(g + 1) * HEAD_DIM], p_ref[g],
                     preferred_element_type=F32) * inv[:, g * BLOCK:(g + 1) * BLOCK]
             for g in range(N_KV_HEADS)], axis=0)
        att_ref[jb, r * KV_DIM:(r + 1) * KV_DIM, :] = out.astype(BF16)

    nitems = nblk * GROUP
    blocks_per_chunk = OPROJ_ROWS // BLOCK

    def out_chunk(c):
        rows = slice(c * OPROJ_ROWS, (c + 1) * OPROJ_ROWS)
        att = jnp.concatenate([att_ref[c * blocks_per_chunk + j]
                               for j in range(blocks_per_chunk)], axis=1)
        mix = lax.dot_general(att, wo_ref[...], _TN, preferred_element_type=F32)
        o_ref[0, rows] = _layer_norm(DEEPNORM_ALPHA * x_ref[0, rows] + mix,
                                     ln_ref[0:1], ln_ref[1:2])

    for t in range(-1, nitems + 1):
        if 0 <= t - 1 < nitems:
            stage_c(t - 1)
        if 0 <= t < nitems:
            stage_b(t)
        if 0 <= t + 1 < nitems:
            stage_a(t + 1)
    for c in range(tm // OPROJ_ROWS):
        out_chunk(c)


def _attn_call(x, qt, k, vt, bias, sink, wo, ln, tm):
    B, S, D = x.shape
    nt = S // tm
    nblk = tm // BLOCK
    nsb = S // BLOCK
    main = lambda b, i: (b, i, 0)
    kprev = lambda b, i: (b, jnp.maximum(i * nblk - 1, 0), 0)
    knext = lambda b, i: (b, jnp.minimum((i + 1) * nblk, nsb - 1), 0)
    vprev = lambda b, i: (b, jnp.maximum(i * nblk - 1, 0), 0, 0)
    vnext = lambda b, i: (b, jnp.minimum((i + 1) * nblk, nsb - 1), 0, 0)
    return pl.pallas_call(
        functools.partial(_attn_kernel, tm=tm, nt=nt),
        grid=(B, nt),
        in_specs=[
            pl.BlockSpec((1, tm, D), main),
            pl.BlockSpec((1, nblk, D, BLOCK), lambda b, i: (b, i, 0, 0)),
            pl.BlockSpec((1, tm, KV_DIM), main),
            pl.BlockSpec((1, BLOCK, KV_DIM), kprev),
            pl.BlockSpec((1, BLOCK, KV_DIM), knext),
            pl.BlockSpec((1, nblk, KV_DIM, BLOCK), lambda b, i: (b, i, 0, 0)),
            pl.BlockSpec((1, 1, KV_DIM, BLOCK), vprev),
            pl.BlockSpec((1, 1, KV_DIM, BLOCK), vnext),
            _resident(bias.shape), _resident(sink.shape), _resident(wo.shape), _resident(ln.shape),
        ],
        out_specs=pl.BlockSpec((1, tm, D), main),
        out_shape=jax.ShapeDtypeStruct(x.shape, F32),
        scratch_shapes=[
            pltpu.VMEM((tm + 2 * BLOCK, KV_DIM), BF16),
            pltpu.VMEM((nblk + 2, KV_DIM, BLOCK), BF16),
            pltpu.VMEM((nblk, D, BLOCK), BF16),
            pltpu.VMEM((N_KV_HEADS, 3 * BLOCK, BLOCK), F32),
            pltpu.VMEM((N_KV_HEADS, 3 * BLOCK, BLOCK), BF16),
            pltpu.VMEM((1, N_KV_HEADS * BLOCK), F32),
        ],
        compiler_params=_params(),
        name="window_attn",
    )(x, qt, k, k, k, vt, vt, vt, bias, sink, wo, ln)


def _attn_bias():
    slopes = (2.0 ** (-8.0 * np.arange(1, N_HEADS + 1) / N_HEADS)).astype(np.float32)
    slopes = slopes.reshape(N_KV_HEADS, GROUP)
    kpos = np.arange(3 * BLOCK)[:, None] - BLOCK
    qpos = np.arange(BLOCK)[None, :]
    dist = np.abs(qpos - kpos)
    band = dist <= BLOCK
    out = np.empty((3, GROUP, N_KV_HEADS, 3 * BLOCK, BLOCK), np.float32)
    for variant in range(3):
        valid = band
        if variant == 0:
            valid = valid & (kpos >= 0)
        if variant == 2:
            valid = valid & (kpos < BLOCK)
        for r in range(GROUP):
            for g in range(N_KV_HEADS):
                out[variant, r, g] = np.where(valid, -slopes[g, r] * dist.astype(np.float32),
                                              np.float32(NEG_BIG))
    return jnp.asarray(out)


def _attn_weights(w_q, w_kv, w_o, sink):
    wqt = w_q.reshape(D_MODEL, N_KV_HEADS, GROUP, HEAD_DIM).transpose(2, 1, 3, 0)
    wqt = wqt.reshape(D_MODEL, D_MODEL).astype(BF16)
    wk = w_kv[:, :KV_DIM].astype(BF16)
    wvt = w_kv[:, KV_DIM:].T.astype(BF16)
    wo = w_o.reshape(N_KV_HEADS, GROUP, HEAD_DIM, D_MODEL).transpose(1, 0, 2, 3)
    wo = wo.reshape(D_MODEL, D_MODEL).astype(BF16)
    sk = sink.astype(F32).reshape(N_KV_HEADS, GROUP).T
    sk = jnp.repeat(sk, BLOCK, axis=1)[:, None, :]
    return wqt, wk, wvt, wo, sk


def _trunk(x, layers, tm):
    for kind, mixer, ffn in layers:
        if kind == "conv":
            x = _conv_call(x, *mixer, tm)
        else:
            wqt, wk, wvt, wo, sk, bias, ln = mixer
            qt, k, vt = _qkv_call(x, wqt, wk, wvt, tm)
            x = _attn_call(x, qt, k, vt, bias, sk, wo, ln, tm)
        x = _ffn_call(x, *ffn, tm)
    return x


def kernel(x_prompt, x_sample, conv_w_pw1, conv_b_pw1, conv_w_dw, conv_b_dw, conv_norm_g,
           conv_norm_b, conv_w_pw2, conv_b_pw2, attn_w_q, attn_w_kv, attn_w_o, attn_sink,
           ffn_w_up, ffn_w_dw, ffn_b_dw, ffn_w_down, ln_mix_g, ln_mix_b, ln_ffn_g, ln_ffn_b):
    bias = _attn_bias()
    layers = []
    for i in range(DEPTH):
        j = i // 2
        if i % 2 == 0:
            wa = conv_w_pw1[j][:, :D_MODEL].astype(BF16)
            wgt = conv_w_pw1[j][:, D_MODEL:].astype(BF16)
            dw = jnp.concatenate([conv_w_dw[j], conv_b_dw[j][None]], axis=0)
            dw = dw.reshape(CONV_WIDTH + 1, N_SLABS, LANES).transpose(1, 0, 2)
            vec = jnp.stack([conv_b_pw1[j][:D_MODEL], conv_b_pw1[j][D_MODEL:], conv_b_dw[j],
                             conv_norm_g[j], conv_norm_b[j], conv_b_pw2[j],
                             ln_mix_g[i], ln_mix_b[i]], axis=0)
            mixer = (wa, wgt, dw, conv_w_pw2[j].astype(BF16), vec)
            kind = "conv"
        else:
            wqt, wk, wvt, wo, sk = _attn_weights(attn_w_q[j], attn_w_kv[j], attn_w_o[j],
                                                 attn_sink[j])
            ln = jnp.stack([ln_mix_g[i], ln_mix_b[i]], axis=0)
            mixer = (wqt, wk, wvt, wo, sk, bias, ln)
            kind = "attn"
        ffn = _ffn_weights(ffn_w_up[i], ffn_w_dw[i], ffn_b_dw[i], ffn_w_down[i])
        ffn = ffn + (jnp.stack([ln_ffn_g[i], ln_ffn_b[i]], axis=0),)
        layers.append((kind, mixer, ffn))
    return (_trunk(x_prompt, layers, ROW_TILE), _trunk(x_sample, layers, ROW_TILE))
```
